```python
import jax, jax.numpy as jnp
from jax import lax
import numpy as np

D_MODEL = 1024
BATCH = 16
SEQ = 256
DEPTH = 1
DEC_BATCH = 8
DEC_SEQ = 2048
PAST_LEN = 256

GRID_W = 64
W_A = D_MODEL // 2
DK = 128
DV = 128
H_A = W_A // DV
W_B = D_MODEL - W_A
WG = 128
G_B = W_B // WG
SGU_CHUNK = 128
CONV_K = 5
DN_CHUNK = 64
D_FF = -(-8 * D_MODEL // (3 * 256)) * 256
N_MOD = 6
IN_W = 4 * W_A + 2 * W_B + 4 * H_A
EPS = 1e-6

kernel_name = "hybrid_deltanet_sgu_diffusion_step"


def rmsnorm(x, g):
    xf = x.astype(jnp.float32)
    y = xf * lax.rsqrt(jnp.mean(xf * xf, axis=-1, keepdims=True) + EPS)
    return (y * g.astype(jnp.float32)).astype(x.dtype)


def layernorm(x, g, b):
    xf = x.astype(jnp.float32)
    mu = jnp.mean(xf, axis=-1, keepdims=True)
    var = jnp.mean(jnp.square(xf - mu), axis=-1, keepdims=True)
    y = (xf - mu) * lax.rsqrt(var + EPS)
    return (y * g.astype(jnp.float32) + b.astype(jnp.float32)).astype(x.dtype)


def l2norm(x):
    xf = x.astype(jnp.float32)
    return (xf * lax.rsqrt(jnp.sum(xf * xf, axis=-1, keepdims=True) + EPS)).astype(x.dtype)


def short_conv(x, w):
    return lax.conv_general_dilated(
        x, w[:, None, :].astype(x.dtype), window_strides=(1,),
        padding=[(CONV_K // 2, CONV_K // 2)],
        dimension_numbers=("NWC", "WIO", "NWC"),
        feature_group_count=x.shape[-1])


def _to_chunks(x, n):
    b, t, h = x.shape[:3]
    x = x.reshape((b, n, DN_CHUNK, h) + x.shape[3:])
    return jnp.moveaxis(x, (1, 3), (0, 2))


def gated_delta_rule(q, k, v, g, beta, s0):
    b, t, h, _ = q.shape
    n = t // DN_CHUNK
    f32 = jnp.float32
    q, k, v, g, beta = (_to_chunks(a.astype(f32), n) for a in (q, k, v, g, beta))
    gc = jnp.cumsum(g, axis=-1)
    idx = jnp.arange(DN_CHUNK)
    incl = idx[:, None] >= idx[None, :]
    strict = idx[:, None] > idx[None, :]
    decay = jnp.exp(jnp.where(incl, gc[..., :, None] - gc[..., None, :], -jnp.inf))
    k_beta = k * beta[..., None]
    nmat = jnp.where(strict, jnp.einsum("nbhik,nbhjk->nbhij", k_beta, k) * decay, 0.0)
    eye = jnp.eye(DN_CHUNK, dtype=f32)
    tmat = lax.linalg.triangular_solve(eye + nmat, jnp.broadcast_to(eye, nmat.shape),
                                       left_side=True, lower=True, unit_diagonal=True)
    u = jnp.einsum("nbhij,nbhjd->nbhid", tmat, v * beta[..., None])
    w = jnp.einsum("nbhij,nbhjk->nbhik", tmat, k_beta * jnp.exp(gc)[..., None])
    qk = jnp.einsum("nbhik,nbhjk->nbhij", q, k) * decay
    q_dec = q * jnp.exp(gc)[..., None]
    k_dec = k * jnp.exp(gc[..., -1:] - gc)[..., None]
    g_last = jnp.exp(gc[..., -1])

    def step(s, inp):
        u_i, w_i, qk_i, q_i, k_i, gl_i = inp
        v_new = u_i - jnp.einsum("bhck,bhkd->bhcd", w_i, s)
        o_i = jnp.einsum("bhck,bhkd->bhcd", q_i, s) + jnp.einsum("bhij,bhjd->bhid", qk_i, v_new)
        s = s * gl_i[..., None, None] + jnp.einsum("bhck,bhcd->bhkd", k_i, v_new)
        return s, o_i

    s_fin, o = lax.scan(step, s0.astype(f32), (u, w, qk, q_dec, k_dec, g_last))
    o = jnp.moveaxis(o, (0, 2), (1, 3)).reshape(b, t, h, -1)
    return o, s_fin


def chunk_sgu(u, v, ln_g, ln_b, w_s, b_s):
    bsz, t, _ = v.shape
    n = t // SGU_CHUNK
    vh = layernorm(v, ln_g, ln_b).reshape(bsz, n, SGU_CHUNK, G_B, WG)
    mixed = jnp.einsum("gts,bnsgc->bntgc", w_s.astype(vh.dtype), vh) + jnp.transpose(b_s)[None, None, :, :, None]
    return u * mixed.reshape(bsz, t, W_B)


def hybrid_mixer(h, s_f0, s_b0, w_in, conv_w, a_log, dt_bias, dn_norm_g,
                 sgu_ln_g, sgu_ln_b, sgu_w, sgu_b, w_out):
    bsz, t, _ = h.shape
    proj = h @ w_in
    qkv, z, uv, beta_logit, a_logit = jnp.split(
        proj, [3 * W_A, 4 * W_A, 4 * W_A + 2 * W_B, 4 * W_A + 2 * W_B + 2 * H_A], axis=-1)
    qkv = jax.nn.silu(short_conv(qkv, conv_w))
    q, k, v = jnp.split(qkv, 3, axis=-1)
    q = l2norm(q.reshape(bsz, t, H_A, DK)) * (DK ** -0.5)
    k = l2norm(k.reshape(bsz, t, H_A, DK))
    v = v.reshape(bsz, t, H_A, DV)
    beta = jax.nn.sigmoid(beta_logit.astype(jnp.float32)).reshape(bsz, t, 2, H_A)
    g = -jnp.exp(a_log.astype(jnp.float32)) * jax.nn.softplus(
        a_logit.astype(jnp.float32).reshape(bsz, t, 2, H_A) + dt_bias.astype(jnp.float32))
    o_f, s_f = gated_delta_rule(q, k, v, g[:, :, 0], beta[:, :, 0], s_f0)
    o_b, s_b = gated_delta_rule(jnp.flip(q, 1), jnp.flip(k, 1), jnp.flip(v, 1),
                                jnp.flip(g[:, :, 1], 1), jnp.flip(beta[:, :, 1], 1), s_b0)
    o = o_f + jnp.flip(o_b, 1)
    o = rmsnorm(o, dn_norm_g) * jax.nn.silu(z.reshape(bsz, t, H_A, DV).astype(jnp.float32))
    y_a = o.reshape(bsz, t, W_A).astype(h.dtype)
    u, vb = jnp.split(jax.nn.gelu(uv), 2, axis=-1)
    y_b = chunk_sgu(u, vb, sgu_ln_g, sgu_ln_b, sgu_w, sgu_b)
    y = jnp.concatenate([y_a, y_b.astype(h.dtype)], axis=-1) @ w_out
    return y, s_f, s_b


def swiglu(h, w_gate, w_up, w_down):
    return (jax.nn.silu(h @ w_gate) * (h @ w_up)) @ w_down


def trunk_layer(x, c, s_f0, s_b0, w_ada, b_ada, norm1_g, norm2_g, w_in, conv_w, a_log,
                dt_bias, dn_norm_g, sgu_ln_g, sgu_ln_b, sgu_w, sgu_b, w_out,
                w_gate, w_up, w_down):
    mod = (jax.nn.silu(c) @ w_ada + b_ada)[:, None, :]
    sh1, sc1, g1, sh2, sc2, g2 = jnp.split(mod, N_MOD, axis=-1)
    h = rmsnorm(x, norm1_g) * (1 + sc1) + sh1
    y, s_f, s_b = hybrid_mixer(h, s_f0, s_b0, w_in, conv_w, a_log, dt_bias, dn_norm_g,
                               sgu_ln_g, sgu_ln_b, sgu_w, sgu_b, w_out)
    x = x + g1 * y
    h = rmsnorm(x, norm2_g) * (1 + sc2) + sh2
    x = x + g2 * swiglu(h, w_gate, w_up, w_down)
    return x, s_f, s_b


def setup_inputs(seed: int = 0) -> dict:
    key = jax.random.key(seed)
    ks = jax.random.split(key, 26)
    f32 = jnp.float32
    nrm = lambda k, shape, s: jax.random.normal(k, shape, f32) * s
    dt = jnp.exp(jax.random.uniform(ks[10], (DEPTH, 2, H_A), f32, np.log(1e-3), np.log(1e-1)))
    return {
        "x_prompt": nrm(ks[0], (BATCH, SEQ, D_MODEL), 1.0),
        "x_sample": nrm(ks[1], (DEC_BATCH, DEC_SEQ, D_MODEL), 1.0),
        "state_fwd": nrm(ks[2], (DEC_BATCH, DEPTH, H_A, DK, DV), DK ** -0.5),
        "state_bwd": nrm(ks[3], (DEC_BATCH, DEPTH, H_A, DK, DV), DK ** -0.5),
        "c": nrm(ks[4], (DEC_BATCH, D_MODEL), 1.0),
        "c_ctx": nrm(ks[5], (D_MODEL,), 1.0),
        "w_ada": nrm(ks[6], (DEPTH, D_MODEL, N_MOD * D_MODEL), 0.5 * D_MODEL ** -0.5),
        "b_ada": nrm(ks[7], (DEPTH, N_MOD * D_MODEL), 0.02),
        "norm1_g": 1.0 + nrm(ks[8], (DEPTH, D_MODEL), 0.02),
        "norm2_g": 1.0 + nrm(ks[9], (DEPTH, D_MODEL), 0.02),
        "w_in": nrm(ks[11], (DEPTH, D_MODEL, IN_W), D_MODEL ** -0.5),
        "conv_w": nrm(ks[12], (DEPTH, CONV_K, 3 * W_A), CONV_K ** -0.5),
        "a_log": jnp.log(jax.random.uniform(ks[13], (DEPTH, 2, H_A), f32, 1.0, 16.0)),
        "dt_bias": dt + jnp.log(-jnp.expm1(-dt)),
        "dn_norm_g": 1.0 + nrm(ks[14], (DEPTH, DV), 0.02),
        "sgu_ln_g": 1.0 + nrm(ks[15], (DEPTH, W_B), 0.02),
        "sgu_ln_b": nrm(ks[16], (DEPTH, W_B), 0.02),
        "sgu_w": nrm(ks[17], (DEPTH, G_B, SGU_CHUNK, SGU_CHUNK), 0.5 * SGU_CHUNK ** -0.5),
        "sgu_b": 1.0 + nrm(ks[18], (DEPTH, G_B, SGU_CHUNK), 0.01),
        "w_out": nrm(ks[19], (DEPTH, D_MODEL, D_MODEL), D_MODEL ** -0.5),
        "w_gate": nrm(ks[20], (DEPTH, D_MODEL, D_FF), D_MODEL ** -0.5),
        "w_up": nrm(ks[21], (DEPTH, D_MODEL, D_FF), D_MODEL ** -0.5),
        "w_down": nrm(ks[22], (DEPTH, D_FF, D_MODEL), D_FF ** -0.5),
        "final_g": 1.0 + nrm(ks[23], (D_MODEL,), 0.02),
    }


def reference(x_prompt, x_sample, state_fwd, state_bwd, c, c_ctx, w_ada, b_ada, norm1_g,
              norm2_g, w_in, conv_w, a_log, dt_bias, dn_norm_g, sgu_ln_g, sgu_ln_b, sgu_w,
              sgu_b, w_out, w_gate, w_up, w_down, final_g):
    s_zero = jnp.zeros((x_prompt.shape[0], H_A, DK, DV), jnp.float32)
    c_prompt = c_ctx[None, :]
    xp, xs = x_prompt, x_sample
    new_f, new_b = [], []
    for l in range(DEPTH):
        params = (w_ada[l], b_ada[l], norm1_g[l], norm2_g[l], w_in[l], conv_w[l], a_log[l],
                  dt_bias[l], dn_norm_g[l], sgu_ln_g[l], sgu_ln_b[l], sgu_w[l], sgu_b[l],
                  w_out[l], w_gate[l], w_up[l], w_down[l])
        xp, sf, sb = trunk_layer(xp, c_prompt, s_zero, s_zero, *params)
        new_f.append(sf)
        new_b.append(sb)
        xs, _, _ = trunk_layer(xs, c, state_fwd[:, l], state_bwd[:, l], *params)
    y_prompt = rmsnorm(xp, final_g)
    y_sample = rmsnorm(xs, final_g)
    new_state_fwd = jnp.stack(new_f, axis=1).astype(x_prompt.dtype)
    new_state_bwd = jnp.stack(new_b, axis=1).astype(x_prompt.dtype)
    return (y_prompt, y_sample, new_state_fwd, new_state_bwd)
```

```python
import functools

import jax
import jax.numpy as jnp
from jax import lax
from jax.experimental import pallas as pl
from jax.experimental.pallas import tpu as pltpu

F32 = jnp.float32
BF16 = jnp.bfloat16

D_MODEL = 1024
W_A = D_MODEL // 2
DK = 128
DV = 128
H_A = W_A // DV
W_B = D_MODEL - W_A
WG = 128
G_B = W_B // WG
SGU_CHUNK = 128
CONV_K = 5
DN_CHUNK = 64
D_FF = 2816
N_MOD = 6
EPS = 1e-6

MOD_ROWS = 16
GATE_W = 128
INV_BASE = 16
FF_CHUNK = 256
VMEM_LIMIT = 56 * 1024 * 1024


def _sigmoid(x):
    return 1.0 / (1.0 + jnp.exp(-x))


def _silu(x):
    return x * _sigmoid(x)


def _gelu_tanh(x):
    c = 0.7978845608028654
    return 0.5 * x * (1.0 + jnp.tanh(c * (x + 0.044715 * (x * x * x))))


def _softplus(x):
    return jnp.maximum(x, 0.0) + jnp.log1p(jnp.exp(-jnp.abs(x)))


def _dot(a, b):
    return jnp.dot(a, b, preferred_element_type=F32)


def _dot_nt(a, b):
    return lax.dot_general(a, b, (((1,), (1,)), ((), ())), preferred_element_type=F32)


def _split3(x):
    hi = x.astype(BF16)
    r1 = x - hi.astype(F32)
    mid = r1.astype(BF16)
    lo = (r1 - mid.astype(F32)).astype(BF16)
    return hi, mid, lo


def _dot_exact_lhs(a_bf16, x):
    hi, mid, lo = _split3(x)
    return _dot(a_bf16, hi) + _dot(a_bf16, mid) + _dot(a_bf16, lo)


def _mm_split(a, b):
    a_hi = a.astype(BF16)
    a_lo = (a - a_hi.astype(F32)).astype(BF16)
    b_hi = b.astype(BF16)
    b_lo = (b - b_hi.astype(F32)).astype(BF16)
    return _dot(a_hi, b_hi) + _dot(a_hi, b_lo) + _dot(a_lo, b_hi)


def _ada_kernel(c_ref, w_ref, b_ref, o_ref):
    s = _silu(c_ref[...])
    o_ref[...] = _dot(s.astype(BF16), w_ref[...].astype(BF16)) + b_ref[...]


def _ada(cc, w_ada, b_ada):
    n_out = w_ada.shape[1]
    bn = D_MODEL
    return pl.pallas_call(
        _ada_kernel,
        grid=(n_out // bn,),
        in_specs=[
            pl.BlockSpec((MOD_ROWS, D_MODEL), lambda j: (0, 0)),
            pl.BlockSpec((D_MODEL, bn), lambda j: (0, j)),
            pl.BlockSpec((1, bn), lambda j: (0, j)),
        ],
        out_specs=pl.BlockSpec((MOD_ROWS, bn), lambda j: (0, j)),
        out_shape=jax.ShapeDtypeStruct((MOD_ROWS, n_out), F32),
        compiler_params=pltpu.CompilerParams(vmem_limit_bytes=VMEM_LIMIT),
        name="ada",
    )(cc, w_ada, b_ada.reshape(1, n_out))


def _inproj_kernel(x_ref, mod_ref, g1_ref, wm_ref, wba_ref, lng_ref, lnb_ref, ws_ref, bs_ref,
                   qkv_ref, z_ref, ba_ref, yb_ref, *, tm):
    x = x_ref[...]
    ms = jnp.mean(x * x, axis=-1, keepdims=True)
    xn = x * lax.rsqrt(ms + EPS) * g1_ref[...]
    h = xn * (1.0 + mod_ref[1:2, :]) + mod_ref[0:1, :]
    hb = h.astype(BF16)
    qkv_ref[...] = _dot(hb, wm_ref[:, 0:3 * W_A])
    z_ref[...] = _dot(hb, wm_ref[:, 3 * W_A:4 * W_A])
    ba_ref[...] = _dot(hb, wba_ref[...])
    u = _gelu_tanh(_dot(hb, wm_ref[:, 4 * W_A:4 * W_A + W_B]))
    vb = _gelu_tanh(_dot(hb, wm_ref[:, 4 * W_A + W_B:4 * W_A + 2 * W_B]))
    mu = jnp.mean(vb, axis=-1, keepdims=True)
    vc = vb - mu
    var = jnp.mean(vc * vc, axis=-1, keepdims=True)
    vh = (vc * lax.rsqrt(var + EPS) * lng_ref[...] + lnb_ref[...]).astype(BF16)
    for c in range(tm // SGU_CHUNK):
        r = slice(c * SGU_CHUNK, (c + 1) * SGU_CHUNK)
        for g in range(G_B):
            l = slice(g * WG, (g + 1) * WG)
            mixed = _dot(ws_ref[g], vh[r, l]) + bs_ref[:, g:g + 1]
            yb_ref[r, l] = (u[r, l] * mixed).astype(BF16)


def _inproj(x2d, mod, g1, w_main, w_ba, ln_g, ln_b, w_s, b_s_t, *, seq, row0, row_stride, tm):
    n_tok = x2d.shape[0]
    tiles_per_seq = seq // tm

    def mod_idx(i):
        return (row0 + (i // tiles_per_seq) * row_stride, 0, 0)

    const2 = lambda i: (0, 0)
    tok = lambda i: (i, 0)
    return pl.pallas_call(
        functools.partial(_inproj_kernel, tm=tm),
        grid=(n_tok // tm,),
        in_specs=[
            pl.BlockSpec((tm, D_MODEL), tok),
            pl.BlockSpec((None, N_MOD, D_MODEL), mod_idx),
            pl.BlockSpec((1, D_MODEL), const2),
            pl.BlockSpec(w_main.shape, const2),
            pl.BlockSpec(w_ba.shape, const2),
            pl.BlockSpec((1, W_B), const2),
            pl.BlockSpec((1, W_B), const2),
            pl.BlockSpec(w_s.shape, lambda i: (0, 0, 0)),
            pl.BlockSpec(b_s_t.shape, const2),
        ],
        out_specs=[
            pl.BlockSpec((tm, 3 * W_A), tok),
            pl.BlockSpec((tm, W_A), tok),
            pl.BlockSpec((tm, GATE_W), tok),
            pl.BlockSpec((tm, W_B), tok),
        ],
        out_shape=[
            jax.ShapeDtypeStruct((n_tok, 3 * W_A), F32),
            jax.ShapeDtypeStruct((n_tok, W_A), F32),
            jax.ShapeDtypeStruct((n_tok, GATE_W), F32),
            jax.ShapeDtypeStruct((n_tok, W_B), BF16),
        ],
        compiler_params=pltpu.CompilerParams(vmem_limit_bytes=VMEM_LIMIT),
        name="inproj",
    )(x2d, mod, g1, w_main, w_ba, ln_g, ln_b, w_s, b_s_t)


def _tri_inv(a, lower):
    n = DN_CHUNK
    row = lax.broadcasted_iota(jnp.int32, (n, n), 0)
    col = lax.broadcasted_iota(jnp.int32, (n, n), 1)
    blocks = []
    for rb in range(n // INV_BASE):
        r0 = rb * INV_BASE
        rows = lax.broadcasted_iota(jnp.int32, (INV_BASE, n), 0) + r0
        cols = lax.broadcasted_iota(jnp.int32, (INV_BASE, n), 1)
        nb = a[r0:r0 + INV_BASE, :]
        x = (cols == rows).astype(F32)
        steps = range(INV_BASE - 1) if lower else range(INV_BASE - 1, 0, -1)
        for j in steps:
            x = x - nb[:, r0 + j:r0 + j + 1] * x[j:j + 1, :]
        blocks.append(x)
    d = jnp.concatenate(blocks, axis=0)
    bs = INV_BASE
    while bs < n:
        same = (row // (2 * bs)) == (col // (2 * bs))
        r_hi = ((row // bs) % 2) == 1
        c_hi = ((col // bs) % 2) == 1
        if lower:
            cmask = same & r_hi & jnp.logical_not(c_hi)
        else:
            cmask = same & jnp.logical_not(r_hi) & c_hi
        c = jnp.where(cmask, a, 0.0)
        d = d - _mm_split(_mm_split(d, c), d)
        bs *= 2
    return d


def _dn_kernel(q_ref, k_ref, v_ref, z_ref, ba_ref, cwq_ref, cwk_ref, cwv_ref, arow_ref, dtrow_ref,
               ng_ref, sf0_ref, sb0_ref,
               ya_ref, sf_ref, sb_ref,
               pad_ref, qs_ref, ks_ref, vs_ref, beta_ref, g_ref,
               u_ref, wq_ref, kdt_ref, qk_ref, et_ref, o_ref, *, seq):
    t = seq
    c_len = DN_CHUNK
    n = t // c_len
    h = pl.program_id(1)
    rb = min(t, 256)

    def pick(x, base):
        out = x[:, base:base + 1]
        for hh in range(1, H_A):
            out = jnp.where(h == hh, x[:, base + hh:base + hh + 1], out)
        return out

    for r in range(0, t, rb):
        ba = ba_ref[r:r + rb, :]
        sig = _sigmoid(ba)
        gall = -jnp.exp(arow_ref[...]) * _softplus(ba + dtrow_ref[...])
        for d in range(2):
            beta_ref[d, r:r + rb, :] = pick(sig, d * H_A)
            g_ref[d, r:r + rb, :] = pick(gall, 2 * H_A + d * H_A)

    zeros8 = jnp.zeros((8, DK), F32)
    pad_ref[0:8, :] = zeros8
    pad_ref[8 + t:16 + t, :] = zeros8

    def conv_silu(x_ref, cw_ref, dst_ref, l2, scale):
        pad_ref[8:8 + t, :] = x_ref[...]
        for r in range(0, t, rb):
            acc = cw_ref[0:1, :] * pad_ref[6 + r:6 + r + rb, :]
            for j in range(1, CONV_K):
                acc = acc + cw_ref[j:j + 1, :] * pad_ref[6 + j + r:6 + j + r + rb, :]
            y = _silu(acc)
            if l2:
                y = y * (lax.rsqrt(jnp.sum(y * y, axis=-1, keepdims=True) + EPS) * scale)
            dst_ref[r:r + rb, :] = y

    conv_silu(q_ref, cwq_ref, qs_ref, True, DK ** -0.5)
    conv_silu(k_ref, cwk_ref, ks_ref, True, 1.0)
    conv_silu(v_ref, cwv_ref, vs_ref, False, 1.0)

    row = lax.broadcasted_iota(jnp.int32, (c_len, c_len), 0)
    col = lax.broadcasted_iota(jnp.int32, (c_len, c_len), 1)
    lower_incl = row >= col
    upper_incl = row <= col
    tri_lo = lower_incl.astype(BF16)
    tri_up = upper_incl.astype(BF16)
    ones_cc = jnp.ones((c_len, c_len), BF16)

    def phase1(c, carry):
        r64 = pl.multiple_of(c * c_len, c_len)
        r128 = pl.multiple_of(c * 2 * c_len, 2 * c_len)
        q = qs_ref[pl.ds(r64, c_len), :]
        k = ks_ref[pl.ds(r64, c_len), :]
        v = vs_ref[pl.ds(r64, c_len), :]
        kb = k.astype(BF16)
        kk = _dot_nt(kb, kb)
        qk = _dot_nt(q.astype(BF16), kb)
        for d in range(2):
            lower = d == 0
            incl = lower_incl if lower else upper_incl
            strict = (row > col) if lower else (row < col)
            tri = tri_lo if lower else tri_up
            tri_t = tri_up if lower else tri_lo
            beta = beta_ref[d, pl.ds(r64, c_len), :]
            g = g_ref[d, pl.ds(r64, c_len), :]
            g_b = jnp.broadcast_to(g, (c_len, DK))
            cum_col = _dot_exact_lhs(tri, g_b)
            cum_row = _dot_exact_lhs(ones_cc, tri_t.astype(F32) * g_b[:, :c_len])
            tot = cum_col[c_len - 1:c_len, :] if lower else cum_col[0:1, :]
            decay = jnp.where(incl, jnp.exp(cum_col[:, :c_len] - cum_row), 0.0)
            a = jnp.where(strict, beta * kk * decay, 0.0)
            tm = _tri_inv(a, lower).astype(BF16)
            e_col = jnp.exp(cum_col)
            u = _dot(tm, (v * beta).astype(BF16))
            w = _dot(tm, (k * (beta * e_col)).astype(BF16))
            u_ref[d, pl.ds(r64, c_len), :] = u
            wq_ref[d, pl.ds(r128, c_len), :] = w.astype(BF16)
            wq_ref[d, pl.ds(r128 + c_len, c_len), :] = (q * e_col).astype(BF16)
            k_dec = k * jnp.exp(tot - cum_col)
            kdt_ref[d, pl.ds(r128, 2 * c_len), :] = k_dec.T.astype(BF16)
            qk_ref[d, pl.ds(r64, c_len), :] = (qk * decay).astype(BF16)
            et_ref[d, pl.ds(pl.multiple_of(c * 8, 8), 8), :] = jnp.exp(jnp.broadcast_to(tot, (8, DK)))
        return carry

    lax.fori_loop(0, n, phase1, 0)

    def phase2(i, carry):
        new = []
        for d in range(2):
            s = carry[d]
            c = i if d == 0 else n - 1 - i
            r64 = pl.multiple_of(c * c_len, c_len)
            r128 = pl.multiple_of(c * 2 * c_len, 2 * c_len)
            sb16 = s.astype(BF16)
            res = _dot(wq_ref[d, pl.ds(r128, 2 * c_len), :], sb16)
            v_new = u_ref[d, pl.ds(r64, c_len), :] - res[0:c_len]
            vb16 = v_new.astype(BF16)
            o_ref[d, pl.ds(r64, c_len), :] = res[c_len:] + _dot(qk_ref[d, pl.ds(r64, c_len), :], vb16)
            et = et_ref[d, pl.ds(pl.multiple_of(c * 8, 8), 8), :]
            new.append(s * et[0:1, :] + _dot(kdt_ref[d, pl.ds(r128, 2 * c_len), :], vb16))
        return tuple(new)

    s_f, s_b = lax.fori_loop(0, n, phase2, (sf0_ref[...], sb0_ref[...]))
    sf_ref[...] = s_f
    sb_ref[...] = s_b

    for r in range(0, t, rb):
        o = o_ref[0, r:r + rb, :] + o_ref[1, r:r + rb, :]
        ms = jnp.mean(o * o, axis=-1, keepdims=True)
        y = o * lax.rsqrt(ms + EPS) * ng_ref[...]
        ya_ref[r:r + rb, :] = (y * _silu(z_ref[r:r + rb, :])).astype(BF16)


def _deltanet(qkv, z, ba, conv_w, a_row, dt_row, norm_g, s_f0, s_b0):
    bsz, t, _ = qkv.shape
    n = t // DN_CHUNK
    col = lambda off: (lambda b, h: (b, 0, off + h))
    const2 = lambda b, h: (0, 0)
    cw = lambda off: (lambda b, h: (0, off + h))
    state = lambda b, h: (b, h, 0, 0)
    return pl.pallas_call(
        functools.partial(_dn_kernel, seq=t),
        grid=(bsz, H_A),
        in_specs=[
            pl.BlockSpec((None, t, DK), col(0)),
            pl.BlockSpec((None, t, DK), col(H_A)),
            pl.BlockSpec((None, t, DV), col(2 * H_A)),
            pl.BlockSpec((None, t, DV), col(0)),
            pl.BlockSpec((None, t, GATE_W), lambda b, h: (b, 0, 0)),
            pl.BlockSpec((CONV_K, DK), cw(0)),
            pl.BlockSpec((CONV_K, DK), cw(H_A)),
            pl.BlockSpec((CONV_K, DV), cw(2 * H_A)),
            pl.BlockSpec((1, GATE_W), const2),
            pl.BlockSpec((1, GATE_W), const2),
            pl.BlockSpec((1, DV), const2),
            pl.BlockSpec((None, None, DK, DV), state),
            pl.BlockSpec((None, None, DK, DV), state),
        ],
        out_specs=[
            pl.BlockSpec((None, t, DV), col(0)),
            pl.BlockSpec((None, None, DK, DV), state),
            pl.BlockSpec((None, None, DK, DV), state),
        ],
        out_shape=[
            jax.ShapeDtypeStruct((bsz, t, W_A), BF16),
            jax.ShapeDtypeStruct((bsz, H_A, DK, DV), F32),
            jax.ShapeDtypeStruct((bsz, H_A, DK, DV), F32),
        ],
        scratch_shapes=[
            pltpu.VMEM((t + 16, DK), F32),
            pltpu.VMEM((t, DK), F32),
            pltpu.VMEM((t, DK), F32),
            pltpu.VMEM((t, DV), F32),
            pltpu.VMEM((2, t, 1), F32),
            pltpu.VMEM((2, t, 1), F32),
            pltpu.VMEM((2, t, DV), F32),
            pltpu.VMEM((2, 2 * t, DK), BF16),
            pltpu.VMEM((2, 2 * t, DN_CHUNK), BF16),
            pltpu.VMEM((2, t, DN_CHUNK), BF16),
            pltpu.VMEM((2, n * 8, DV), F32),
            pltpu.VMEM((2, t, DV), F32),
        ],
        compiler_params=pltpu.CompilerParams(vmem_limit_bytes=VMEM_LIMIT),
        name="deltanet",
    )(qkv, qkv, qkv, z, ba, conv_w, conv_w, conv_w, a_row, dt_row, norm_g, s_f0, s_b0)


def _outffn_kernel(x_ref, ya_ref, yb_ref, mod_ref, n2_ref, fg_ref, wo_ref, wg_ref, wu_ref, wd_ref, o_ref):
    y = _dot(ya_ref[...], wo_ref[0:W_A, :]) + _dot(yb_ref[...], wo_ref[W_A:D_MODEL, :])
    x1 = x_ref[...] + mod_ref[2:3, :] * y
    ms = jnp.mean(x1 * x1, axis=-1, keepdims=True)
    h = (x1 * lax.rsqrt(ms + EPS) * n2_ref[...]) * (1.0 + mod_ref[4:5, :]) + mod_ref[3:4, :]
    hb = h.astype(BF16)
    acc = None
    for c in range(D_FF // FF_CHUNK):
        l = slice(c * FF_CHUNK, (c + 1) * FF_CHUNK)
        act = _silu(_dot(hb, wg_ref[:, l])) * _dot(hb, wu_ref[:, l])
        part = _dot(act.astype(BF16), wd_ref[l, :])
        acc = part if acc is None else acc + part
    x2 = x1 + mod_ref[5:6, :] * acc
    ms2 = jnp.mean(x2 * x2, axis=-1, keepdims=True)
    o_ref[...] = x2 * lax.rsqrt(ms2 + EPS) * fg_ref[...]


def _outffn(x2d, ya, yb, mod, n2, fg, w_out, w_gate, w_up, w_down, *, seq, row0, row_stride, tm):
    n_tok = x2d.shape[0]
    tiles_per_seq = seq // tm

    def mod_idx(i):
        return (row0 + (i // tiles_per_seq) * row_stride, 0, 0)

    const2 = lambda i: (0, 0)
    tok = lambda i: (i, 0)
    resident = lambda w: pl.BlockSpec(w.shape, const2, pipeline_mode=pl.Buffered(1))
    return pl.pallas_call(
        _outffn_kernel,
        grid=(n_tok // tm,),
        in_specs=[
            pl.BlockSpec((tm, D_MODEL), tok),
            pl.BlockSpec((tm, W_A), tok),
            pl.BlockSpec((tm, W_B), tok),
            pl.BlockSpec((None, N_MOD, D_MODEL), mod_idx),
            pl.BlockSpec((1, D_MODEL), const2),
            pl.BlockSpec((1, D_MODEL), const2),
            resident(w_out), resident(w_gate), resident(w_up), resident(w_down),
        ],
        out_specs=pl.BlockSpec((tm, D_MODEL), tok),
        out_shape=jax.ShapeDtypeStruct((n_tok, D_MODEL), F32),
        compiler_params=pltpu.CompilerParams(vmem_limit_bytes=VMEM_LIMIT),
        name="outffn",
    )(x2d, ya, yb, mod, n2, fg, w_out, w_gate, w_up, w_down)


def _trunk_layer(x, mod, row0, row_stride, s_f0, s_b0, p, final_g):
    bsz, t, _ = x.shape
    x2d = x.reshape(bsz * t, D_MODEL)
    tm = 256
    qkv, z, ba, yb = _inproj(x2d, mod, p["g1"], p["w_main"], p["w_ba"], p["ln_g"], p["ln_b"], p["w_s"],
                             p["b_s_t"], seq=t, row0=row0, row_stride=row_stride, tm=tm)
    ya, s_f, s_b = _deltanet(qkv.reshape(bsz, t, 3 * W_A), z.reshape(bsz, t, W_A), ba.reshape(bsz, t, GATE_W),
                             p["conv_w"], p["a_row"], p["dt_row"], p["dn_g"], s_f0, s_b0)
    out = _outffn(x2d, ya.reshape(bsz * t, W_A), yb, mod, p["g2"], final_g, p["w_out"], p["w_gate"], p["w_up"],
                  p["w_down"], seq=t, row0=row0, row_stride=row_stride, tm=tm)
    return out.reshape(bsz, t, D_MODEL), s_f, s_b


def kernel(x_prompt, x_sample, state_fwd, state_bwd, c, c_ctx, w_ada, b_ada, norm1_g, norm2_g, w_in, conv_w,
           a_log, dt_bias, dn_norm_g, sgu_ln_g, sgu_ln_b, sgu_w, sgu_b, w_out, w_gate, w_up, w_down, final_g):
    depth = w_ada.shape[0]
    assert depth == 1, "the final RMSNorm is fused into the last layer; only depth 1 is supported"
    n_lat = c.shape[0]
    assert 1 + n_lat <= MOD_ROWS
    cc = jnp.zeros((MOD_ROWS, D_MODEL), F32).at[0].set(c_ctx).at[1:1 + n_lat].set(c)
    s_zero = jnp.zeros((x_prompt.shape[0], H_A, DK, DV), F32)
    xp, xs = x_prompt, x_sample
    new_f, new_b = [], []
    gate_pad = jnp.zeros((2 * H_A,), F32)
    for l in range(depth):
        w_in_b = w_in[l].astype(BF16)
        n_main = 4 * W_A + 2 * W_B
        p = dict(
            g1=norm1_g[l].reshape(1, D_MODEL), g2=norm2_g[l].reshape(1, D_MODEL),
            w_main=w_in_b[:, :n_main],
            w_ba=jnp.pad(w_in_b[:, n_main:], ((0, 0), (0, GATE_W - 4 * H_A))),
            ln_g=sgu_ln_g[l].reshape(1, W_B), ln_b=sgu_ln_b[l].reshape(1, W_B),
            w_s=sgu_w[l].astype(BF16), b_s_t=jnp.transpose(sgu_b[l]),
            conv_w=conv_w[l],
            a_row=jnp.pad(jnp.concatenate([gate_pad, a_log[l].reshape(-1)]), (0, GATE_W - 4 * H_A)).reshape(1, GATE_W),
            dt_row=jnp.pad(jnp.concatenate([gate_pad, dt_bias[l].reshape(-1)]), (0, GATE_W - 4 * H_A)).reshape(1, GATE_W),
            dn_g=dn_norm_g[l].reshape(1, DV),
            w_out=w_out[l].astype(BF16), w_gate=w_gate[l].astype(BF16), w_up=w_up[l].astype(BF16),
            w_down=w_down[l].astype(BF16),
        )
        mod = _ada(cc, w_ada[l], b_ada[l]).reshape(MOD_ROWS, N_MOD, D_MODEL)
        fg = final_g.reshape(1, D_MODEL)
        xp, sf, sb = _trunk_layer(xp, mod, 0, 0, s_zero, s_zero, p, fg)
        new_f.append(sf)
        new_b.append(sb)
        xs, _, _ = _trunk_layer(xs, mod, 1, 1, state_fwd[:, l], state_bwd[:, l], p, fg)
    return (xp, xs, jnp.stack(new_f, axis=1), jnp.stack(new_b, axis=1))
```

```python
import functools

import jax
import jax.numpy as jnp
from jax import lax
from jax.experimental import pallas as pl
from jax.experimental.pallas import tpu as pltpu

F32 = jnp.float32
BF16 = jnp.bfloat16

D_MODEL = 1024
W_A = D_MODEL // 2
DK = 128
DV = 128
H_A = W_A // DV
W_B = D_MODEL - W_A
WG = 128
G_B = W_B // WG
SGU_CHUNK = 128
CONV_K = 5
DN_CHUNK = 64
D_FF = 2816
N_MOD = 6
EPS = 1e-6

MOD_ROWS = 16
GATE_W = 128
INV_BASE = 16
FF_CHUNK = 256
P1_BATCH = 4
VMEM_LIMIT = 56 * 1024 * 1024


def _sigmoid(x):
    return 0.5 + 0.5 * jnp.tanh(0.5 * x)


def _silu(x):
    return x * _sigmoid(x)


def _gelu_tanh(x):
    c = 0.7978845608028654
    return 0.5 * x * (1.0 + jnp.tanh(c * (x + 0.044715 * (x * x * x))))


def _softplus(x):
    return jnp.maximum(x, 0.0) + jnp.log(1.0 + jnp.exp(-jnp.abs(x)))


def _dot(a, b):
    return jnp.dot(a, b, preferred_element_type=F32)


def _dot_nt(a, b):
    return lax.dot_general(a, b, (((1,), (1,)), ((), ())), preferred_element_type=F32)


def _split3(x):
    hi = x.astype(BF16)
    r1 = x - hi.astype(F32)
    mid = r1.astype(BF16)
    lo = (r1 - mid.astype(F32)).astype(BF16)
    return hi, mid, lo


def _dot_exact_lhs(a_bf16, x):
    hi, mid, lo = _split3(x)
    return _dot(a_bf16, hi) + _dot(a_bf16, mid) + _dot(a_bf16, lo)


def _mm_split(a, b):
    a_hi = a.astype(BF16)
    a_lo = (a - a_hi.astype(F32)).astype(BF16)
    b_hi = b.astype(BF16)
    b_lo = (b - b_hi.astype(F32)).astype(BF16)
    return _dot(a_hi, b_hi) + _dot(a_hi, b_lo) + _dot(a_lo, b_hi)


def _ada_kernel(c_ref, w_ref, b_ref, o_ref):
    s = _silu(c_ref[...])
    o_ref[...] = _dot(s.astype(BF16), w_ref[...].astype(BF16)) + b_ref[...]


def _ada(cc, w_ada, b_ada):
    n_out = w_ada.shape[1]
    bn = D_MODEL
    return pl.pallas_call(
        _ada_kernel,
        grid=(n_out // bn,),
        in_specs=[
            pl.BlockSpec((MOD_ROWS, D_MODEL), lambda j: (0, 0)),
            pl.BlockSpec((D_MODEL, bn), lambda j: (0, j)),
            pl.BlockSpec((1, bn), lambda j: (0, j)),
        ],
        out_specs=pl.BlockSpec((MOD_ROWS, bn), lambda j: (0, j)),
        out_shape=jax.ShapeDtypeStruct((MOD_ROWS, n_out), F32),
        compiler_params=pltpu.CompilerParams(vmem_limit_bytes=VMEM_LIMIT),
        name="ada",
    )(cc, w_ada, b_ada.reshape(1, n_out))


def _inproj_kernel(x_ref, mod_ref, g1_ref, wm_ref, wba_ref, arow_ref, dtrow_ref, lng_ref, lnb_ref, ws_ref, bs_ref,
                   qkv_ref, z_ref, ba_ref, yb_ref, *, tm):
    x = x_ref[...]
    ms = jnp.mean(x * x, axis=-1, keepdims=True)
    xn = x * lax.rsqrt(ms + EPS) * g1_ref[...]
    h = xn * (1.0 + mod_ref[1:2, :]) + mod_ref[0:1, :]
    hb = h.astype(BF16)
    qkv_ref[...] = _dot(hb, wm_ref[:, 0:3 * W_A])
    z_ref[...] = _dot(hb, wm_ref[:, 3 * W_A:4 * W_A])
    logit = _dot(hb, wba_ref[...])
    lane = lax.broadcasted_iota(jnp.int32, logit.shape, 1)
    g_all = -jnp.exp(arow_ref[...]) * _softplus(logit + dtrow_ref[...])
    ba_ref[...] = jnp.where(lane < 2 * H_A, _sigmoid(logit), g_all)
    u = _gelu_tanh(_dot(hb, wm_ref[:, 4 * W_A:4 * W_A + W_B]))
    vb = _gelu_tanh(_dot(hb, wm_ref[:, 4 * W_A + W_B:4 * W_A + 2 * W_B]))
    mu = jnp.mean(vb, axis=-1, keepdims=True)
    vc = vb - mu
    var = jnp.mean(vc * vc, axis=-1, keepdims=True)
    vh = (vc * lax.rsqrt(var + EPS) * lng_ref[...] + lnb_ref[...]).astype(BF16)
    for c in range(tm // SGU_CHUNK):
        r = slice(c * SGU_CHUNK, (c + 1) * SGU_CHUNK)
        for g in range(G_B):
            l = slice(g * WG, (g + 1) * WG)
            mixed = _dot(ws_ref[g], vh[r, l]) + bs_ref[:, g:g + 1]
            yb_ref[r, l] = (u[r, l] * mixed).astype(BF16)


def _inproj(x2d, mod, g1, w_main, w_ba, a_row, dt_row, ln_g, ln_b, w_s, b_s_t, *, seq, row0, row_stride, tm):
    n_tok = x2d.shape[0]
    tiles_per_seq = seq // tm

    def mod_idx(i):
        return (row0 + (i // tiles_per_seq) * row_stride, 0, 0)

    const2 = lambda i: (0, 0)
    tok = lambda i: (i, 0)
    return pl.pallas_call(
        functools.partial(_inproj_kernel, tm=tm),
        grid=(n_tok // tm,),
        in_specs=[
            pl.BlockSpec((tm, D_MODEL), tok),
            pl.BlockSpec((None, N_MOD, D_MODEL), mod_idx),
            pl.BlockSpec((1, D_MODEL), const2),
            pl.BlockSpec(w_main.shape, const2),
            pl.BlockSpec(w_ba.shape, const2),
            pl.BlockSpec((1, GATE_W), const2),
            pl.BlockSpec((1, GATE_W), const2),
            pl.BlockSpec((1, W_B), const2),
            pl.BlockSpec((1, W_B), const2),
            pl.BlockSpec(w_s.shape, lambda i: (0, 0, 0)),
            pl.BlockSpec(b_s_t.shape, const2),
        ],
        out_specs=[
            pl.BlockSpec((tm, 3 * W_A), tok),
            pl.BlockSpec((tm, W_A), tok),
            pl.BlockSpec((tm, GATE_W), tok),
            pl.BlockSpec((tm, W_B), tok),
        ],
        out_shape=[
            jax.ShapeDtypeStruct((n_tok, 3 * W_A), F32),
            jax.ShapeDtypeStruct((n_tok, W_A), F32),
            jax.ShapeDtypeStruct((n_tok, GATE_W), F32),
            jax.ShapeDtypeStruct((n_tok, W_B), BF16),
        ],
        compiler_params=pltpu.CompilerParams(vmem_limit_bytes=VMEM_LIMIT),
        name="inproj",
    )(x2d, mod, g1, w_main, w_ba, a_row, dt_row, ln_g, ln_b, w_s, b_s_t)


def _tri_inv_lower(mats):
    n = DN_CHUNK
    nblk = n // INV_BASE
    assert len(mats) % 2 == 0
    lane1 = lax.broadcasted_iota(jnp.int32, (INV_BASE, n), 1)
    lane2 = lax.broadcasted_iota(jnp.int32, (INV_BASE, 2 * n), 1)
    sub2 = lax.broadcasted_iota(jnp.int32, (INV_BASE, 2 * n), 0)
    blk_masks = [(lane1 // INV_BASE) == rb for rb in range(nblk)]

    def compress(a):
        out = jnp.where(blk_masks[0], a[0:INV_BASE, :], 0.0)
        for rb in range(1, nblk):
            out = out + jnp.where(blk_masks[rb], a[rb * INV_BASE:(rb + 1) * INV_BASE, :], 0.0)
        return out

    packed = [jnp.concatenate([compress(mats[2 * p]), compress(mats[2 * p + 1])], axis=1)
              for p in range(len(mats) // 2)]
    eye = ((lane2 % INV_BASE) == sub2).astype(F32)
    xs = [eye for _ in packed]
    grp = (lane2 // INV_BASE) * INV_BASE
    for j in range(INV_BASE - 1):
        cols = [jnp.take_along_axis(nc, grp + j, axis=1) for nc in packed]
        xs = [x - c * x[j:j + 1, :] for x, c in zip(xs, cols)]

    def expand(xc):
        return jnp.concatenate([jnp.where(m, xc, 0.0) for m in blk_masks], axis=0)

    ds = []
    for x in xs:
        ds.append(expand(x[:, 0:n]))
        ds.append(expand(x[:, n:2 * n]))
    row = lax.broadcasted_iota(jnp.int32, (n, n), 0)
    col = lax.broadcasted_iota(jnp.int32, (n, n), 1)
    bs = INV_BASE
    while bs < n:
        cmask = ((row // (2 * bs)) == (col // (2 * bs))) & (((row // bs) % 2) == 1) & (((col // bs) % 2) == 0)
        cs = [jnp.where(cmask, a, 0.0).astype(BF16) for a in mats]
        dbs = [d.astype(BF16) for d in ds]
        t1 = [_dot(db, c) for db, c in zip(dbs, cs)]
        t2 = [_dot(t.astype(BF16), db) for t, db in zip(t1, dbs)]
        ds = [d - t for d, t in zip(ds, t2)]
        bs *= 2
    return ds


def _dn_kernel(q_ref, k_ref, v_ref, z_ref, ba_ref, cwq_ref, cwk_ref, cwv_ref,
               ng_ref, sf0_ref, sb0_ref,
               ya_ref, sf_ref, sb_ref,
               pad_ref, qs_ref, ks_ref, vs_ref, beta_ref, g_ref,
               u_ref, wq_ref, kdt_ref, qk_ref, et_ref, o_ref, *, seq):
    t = seq
    c_len = DN_CHUNK
    n = t // c_len
    h = pl.program_id(1)
    rb = min(t, 256)

    def pick(x, base):
        out = x[:, base:base + 1]
        for hh in range(1, H_A):
            out = jnp.where(h == hh, x[:, base + hh:base + hh + 1], out)
        return out

    for r in range(0, t, rb):
        gates = ba_ref[r:r + rb, :]
        for d in range(2):
            beta_ref[d, r:r + rb, :] = pick(gates, d * H_A)
            g_ref[d, r:r + rb, :] = pick(gates, 2 * H_A + d * H_A)

    zeros8 = jnp.zeros((8, DK), F32)
    pad_ref[0:8, :] = zeros8
    pad_ref[8 + t:16 + t, :] = zeros8

    def conv_silu(x_ref, cw_ref, dst_ref, l2, scale):
        pad_ref[8:8 + t, :] = x_ref[...]
        for r in range(0, t, rb):
            acc = cw_ref[0:1, :] * pad_ref[6 + r:6 + r + rb, :]
            for j in range(1, CONV_K):
                acc = acc + cw_ref[j:j + 1, :] * pad_ref[6 + j + r:6 + j + r + rb, :]
            y = _silu(acc)
            if l2:
                y = y * (lax.rsqrt(jnp.sum(y * y, axis=-1, keepdims=True) + EPS) * scale)
            dst_ref[r:r + rb, :] = y

    conv_silu(q_ref, cwq_ref, qs_ref, True, DK ** -0.5)
    conv_silu(k_ref, cwk_ref, ks_ref, True, 1.0)
    conv_silu(v_ref, cwv_ref, vs_ref, False, 1.0)

    row = lax.broadcasted_iota(jnp.int32, (c_len, c_len), 0)
    col = lax.broadcasted_iota(jnp.int32, (c_len, c_len), 1)
    lower_incl = row >= col
    upper_incl = row <= col
    lower_strict = row > col
    tri = (lower_incl.astype(BF16), upper_incl.astype(BF16))
    nb = P1_BATCH
    assert n % nb == 0

    def phase1(it, carry):
        cs = [it * nb + i for i in range(nb)]
        r64 = [pl.multiple_of(c * c_len, c_len) for c in cs]
        r128 = [pl.multiple_of(c * 2 * c_len, 2 * c_len) for c in cs]
        r8 = [pl.multiple_of(c * 8, 8) for c in cs]
        items = [(i, d) for i in range(nb) for d in range(2)]
        q = [qs_ref[pl.ds(r, c_len), :] for r in r64]
        k = [ks_ref[pl.ds(r, c_len), :] for r in r64]
        v = [vs_ref[pl.ds(r, c_len), :] for r in r64]
        kb = [x.astype(BF16) for x in k]
        kk = [_dot_nt(x, x) for x in kb]
        qk = [_dot_nt(x.astype(BF16), y) for x, y in zip(q, kb)]
        beta = [beta_ref[d, pl.ds(r64[i], c_len), :] for i, d in items]
        g_b = [jnp.broadcast_to(g_ref[d, pl.ds(r64[i], c_len), :], (c_len, DK)) for i, d in items]
        cum = [_dot_exact_lhs(tri[d], gb) for (i, d), gb in zip(items, g_b)]
        cum_t = [x[:, 0:c_len].T for x in cum]
        dm = [x[:, 0:c_len] - y for x, y in zip(cum, cum_t)]
        dec_sys = [jnp.where(lower_incl, jnp.exp(x if d == 0 else -x), 0.0) for (i, d), x in zip(items, dm)]
        dec_qk = [ds if d == 0 else jnp.where(upper_incl, jnp.exp(x), 0.0)
                  for (i, d), x, ds in zip(items, dm, dec_sys)]
        a = [jnp.where(lower_strict, b * kk[i] * ds, 0.0) for (i, d), b, ds in zip(items, beta, dec_sys)]
        tm = [x.astype(BF16) for x in _tri_inv_lower(a)]
        e_col = [jnp.exp(x) for x in cum]
        rhs = []
        for (i, d), b, e in zip(items, beta, e_col):
            if d == 0:
                rhs.append(jnp.concatenate([v[i] * b, k[i] * (b * e)], axis=1).astype(BF16))
            else:
                rhs.append(jnp.concatenate([v[i], k[i] * e], axis=1).astype(BF16))
        uw = []
        for (i, d), t_, r_, b in zip(items, tm, rhs, beta):
            if d == 0:
                uw.append(_dot(t_, r_))
            else:
                uw.append(b * lax.dot_general(t_, r_, (((0,), (0,)), ((), ())), preferred_element_type=F32))
        tot = [x[c_len - 1:c_len, :] if d == 0 else x[0:1, :] for (i, d), x in zip(items, cum)]
        k_dec_t = [(k[i] * jnp.exp(t_ - x)).T.astype(BF16) for (i, d), x, t_ in zip(items, cum, tot)]
        for idx, (i, d) in enumerate(items):
            u_ref[d, pl.ds(r64[i], c_len), :] = uw[idx][:, 0:DV]
            wq_ref[d, pl.ds(r128[i], c_len), :] = uw[idx][:, DV:DV + DK].astype(BF16)
            wq_ref[d, pl.ds(r128[i] + c_len, c_len), :] = (q[i] * e_col[idx]).astype(BF16)
            kdt_ref[d, pl.ds(r128[i], 2 * c_len), :] = k_dec_t[idx]
            qk_ref[d, pl.ds(r64[i], c_len), :] = (qk[i] * dec_qk[idx]).astype(BF16)
            et_ref[d, pl.ds(r8[i], 8), :] = jnp.exp(jnp.broadcast_to(tot[idx], (8, DK)))
        return carry

    lax.fori_loop(0, n // nb, phase1, 0)

    def phase2(i, carry):
        new = []
        for d in range(2):
            s = carry[d]
            c = i if d == 0 else n - 1 - i
            r64 = pl.multiple_of(c * c_len, c_len)
            r128 = pl.multiple_of(c * 2 * c_len, 2 * c_len)
            sb16 = s.astype(BF16)
            res = _dot(wq_ref[d, pl.ds(r128, 2 * c_len), :], sb16)
            v_new = u_ref[d, pl.ds(r64, c_len), :] - res[0:c_len]
            vb16 = v_new.astype(BF16)
            o_ref[d, pl.ds(r64, c_len), :] = res[c_len:] + _dot(qk_ref[d, pl.ds(r64, c_len), :], vb16)
            et = et_ref[d, pl.ds(pl.multiple_of(c * 8, 8), 8), :]
            new.append(s * et[0:1, :] + _dot(kdt_ref[d, pl.ds(r128, 2 * c_len), :], vb16))
        return tuple(new)

    s_f, s_b = lax.fori_loop(0, n, phase2, (sf0_ref[...], sb0_ref[...]))
    sf_ref[...] = s_f
    sb_ref[...] = s_b

    for r in range(0, t, rb):
        o = o_ref[0, r:r + rb, :] + o_ref[1, r:r + rb, :]
        ms = jnp.mean(o * o, axis=-1, keepdims=True)
        y = o * lax.rsqrt(ms + EPS) * ng_ref[...]
        ya_ref[r:r + rb, :] = (y * _silu(z_ref[r:r + rb, :])).astype(BF16)


def _deltanet(qkv, z, ba, conv_w, norm_g, s_f0, s_b0):
    bsz, t, _ = qkv.shape
    n = t // DN_CHUNK
    col = lambda off: (lambda b, h: (b, 0, off + h))
    const2 = lambda b, h: (0, 0)
    cw = lambda off: (lambda b, h: (0, off + h))
    state = lambda b, h: (b, h, 0, 0)
    return pl.pallas_call(
        functools.partial(_dn_kernel, seq=t),
        grid=(bsz, H_A),
        in_specs=[
            pl.BlockSpec((None, t, DK), col(0)),
            pl.BlockSpec((None, t, DK), col(H_A)),
            pl.BlockSpec((None, t, DV), col(2 * H_A)),
            pl.BlockSpec((None, t, DV), col(0)),
            pl.BlockSpec((None, t, GATE_W), lambda b, h: (b, 0, 0)),
            pl.BlockSpec((CONV_K, DK), cw(0)),
            pl.BlockSpec((CONV_K, DK), cw(H_A)),
            pl.BlockSpec((CONV_K, DV), cw(2 * H_A)),
            pl.BlockSpec((1, DV), const2),
            pl.BlockSpec((None, None, DK, DV), state),
            pl.BlockSpec((None, None, DK, DV), state),
        ],
        out_specs=[
            pl.BlockSpec((None, t, DV), col(0)),
            pl.BlockSpec((None, None, DK, DV), state),
            pl.BlockSpec((None, None, DK, DV), state),
        ],
        out_shape=[
            jax.ShapeDtypeStruct((bsz, t, W_A), BF16),
            jax.ShapeDtypeStruct((bsz, H_A, DK, DV), F32),
            jax.ShapeDtypeStruct((bsz, H_A, DK, DV), F32),
        ],
        scratch_shapes=[
            pltpu.VMEM((t + 16, DK), F32),
            pltpu.VMEM((t, DK), F32),
            pltpu.VMEM((t, DK), F32),
            pltpu.VMEM((t, DV), F32),
            pltpu.VMEM((2, t, 1), F32),
            pltpu.VMEM((2, t, 1), F32),
            pltpu.VMEM((2, t, DV), F32),
            pltpu.VMEM((2, 2 * t, DK), BF16),
            pltpu.VMEM((2, 2 * t, DN_CHUNK), BF16),
            pltpu.VMEM((2, t, DN_CHUNK), BF16),
            pltpu.VMEM((2, n * 8, DV), F32),
            pltpu.VMEM((2, t, DV), F32),
        ],
        compiler_params=pltpu.CompilerParams(vmem_limit_bytes=VMEM_LIMIT),
        name="deltanet",
    )(qkv, qkv, qkv, z, ba, conv_w, conv_w, conv_w, norm_g, s_f0, s_b0)


def _outffn_kernel(x_ref, ya_ref, yb_ref, mod_ref, n2_ref, fg_ref, wo_ref, wg_ref, wu_ref, wd_ref, o_ref):
    y = _dot(ya_ref[...], wo_ref[0:W_A, :]) + _dot(yb_ref[...], wo_ref[W_A:D_MODEL, :])
    x1 = x_ref[...] + mod_ref[2:3, :] * y
    ms = jnp.mean(x1 * x1, axis=-1, keepdims=True)
    h = (x1 * lax.rsqrt(ms + EPS) * n2_ref[...]) * (1.0 + mod_ref[4:5, :]) + mod_ref[3:4, :]
    hb = h.astype(BF16)
    acc = None
    for c in range(D_FF // FF_CHUNK):
        l = slice(c * FF_CHUNK, (c + 1) * FF_CHUNK)
        act = _silu(_dot(hb, wg_ref[:, l])) * _dot(hb, wu_ref[:, l])
        part = _dot(act.astype(BF16), wd_ref[l, :])
        acc = part if acc is None else acc + part
    x2 = x1 + mod_ref[5:6, :] * acc
    ms2 = jnp.mean(x2 * x2, axis=-1, keepdims=True)
    o_ref[...] = x2 * lax.rsqrt(ms2 + EPS) * fg_ref[...]


def _outffn(x2d, ya, yb, mod, n2, fg, w_out, w_gate, w_up, w_down, *, seq, row0, row_stride, tm):
    n_tok = x2d.shape[0]
    tiles_per_seq = seq // tm

    def mod_idx(i):
        return (row0 + (i // tiles_per_seq) * row_stride, 0, 0)

    const2 = lambda i: (0, 0)
    tok = lambda i: (i, 0)
    resident = lambda w: pl.BlockSpec(w.shape, const2, pipeline_mode=pl.Buffered(1))
    return pl.pallas_call(
        _outffn_kernel,
        grid=(n_tok // tm,),
        in_specs=[
            pl.BlockSpec((tm, D_MODEL), tok),
            pl.BlockSpec((tm, W_A), tok),
            pl.BlockSpec((tm, W_B), tok),
            pl.BlockSpec((None, N_MOD, D_MODEL), mod_idx),
            pl.BlockSpec((1, D_MODEL), const2),
            pl.BlockSpec((1, D_MODEL), const2),
            resident(w_out), resident(w_gate), resident(w_up), resident(w_down),
        ],
        out_specs=pl.BlockSpec((tm, D_MODEL), tok),
        out_shape=jax.ShapeDtypeStruct((n_tok, D_MODEL), F32),
        compiler_params=pltpu.CompilerParams(vmem_limit_bytes=VMEM_LIMIT),
        name="outffn",
    )(x2d, ya, yb, mod, n2, fg, w_out, w_gate, w_up, w_down)


def _trunk_layer(x, mod, row0, row_stride, s_f0, s_b0, p, final_g):
    bsz, t, _ = x.shape
    x2d = x.reshape(bsz * t, D_MODEL)
    tm = 256
    qkv, z, ba, yb = _inproj(x2d, mod, p["g1"], p["w_main"], p["w_ba"], p["a_row"], p["dt_row"], p["ln_g"], p["ln_b"], p["w_s"],
                             p["b_s_t"], seq=t, row0=row0, row_stride=row_stride, tm=tm)
    ya, s_f, s_b = _deltanet(qkv.reshape(bsz, t, 3 * W_A), z.reshape(bsz, t, W_A), ba.reshape(bsz, t, GATE_W),
                             p["conv_w"], p["dn_g"], s_f0, s_b0)
    out = _outffn(x2d, ya.reshape(bsz * t, W_A), yb, mod, p["g2"], final_g, p["w_out"], p["w_gate"], p["w_up"],
                  p["w_down"], seq=t, row0=row0, row_stride=row_stride, tm=tm)
    return out.reshape(bsz, t, D_MODEL), s_f, s_b


def kernel(x_prompt, x_sample, state_fwd, state_bwd, c, c_ctx, w_ada, b_ada, norm1_g, norm2_g, w_in, conv_w,
           a_log, dt_bias, dn_norm_g, sgu_ln_g, sgu_ln_b, sgu_w, sgu_b, w_out, w_gate, w_up, w_down, final_g):
    depth = w_ada.shape[0]
    assert depth == 1, "the final RMSNorm is fused into the last layer; only depth 1 is supported"
    n_lat = c.shape[0]
    assert 1 + n_lat <= MOD_ROWS
    cc = jnp.zeros((MOD_ROWS, D_MODEL), F32).at[0].set(c_ctx).at[1:1 + n_lat].set(c)
    s_zero = jnp.zeros((x_prompt.shape[0], H_A, DK, DV), F32)
    xp, xs = x_prompt, x_sample
    new_f, new_b = [], []
    gate_pad = jnp.zeros((2 * H_A,), F32)
    for l in range(depth):
        w_in_b = w_in[l].astype(BF16)
        n_main = 4 * W_A + 2 * W_B
        p = dict(
            g1=norm1_g[l].reshape(1, D_MODEL), g2=norm2_g[l].reshape(1, D_MODEL),
            w_main=w_in_b[:, :n_main],
            w_ba=jnp.pad(w_in_b[:, n_main:], ((0, 0), (0, GATE_W - 4 * H_A))),
            ln_g=sgu_ln_g[l].reshape(1, W_B), ln_b=sgu_ln_b[l].reshape(1, W_B),
            w_s=sgu_w[l].astype(BF16), b_s_t=jnp.transpose(sgu_b[l]),
            conv_w=conv_w[l],
            a_row=jnp.pad(jnp.concatenate([gate_pad, a_log[l].reshape(-1)]), (0, GATE_W - 4 * H_A)).reshape(1, GATE_W),
            dt_row=jnp.pad(jnp.concatenate([gate_pad, dt_bias[l].reshape(-1)]), (0, GATE_W - 4 * H_A)).reshape(1, GATE_W),
            dn_g=dn_norm_g[l].reshape(1, DV),
            w_out=w_out[l].astype(BF16), w_gate=w_gate[l].astype(BF16), w_up=w_up[l].astype(BF16),
            w_down=w_down[l].astype(BF16),
        )
        mod = _ada(cc, w_ada[l], b_ada[l]).reshape(MOD_ROWS, N_MOD, D_MODEL)
        fg = final_g.reshape(1, D_MODEL)
        xp, sf, sb = _trunk_layer(xp, mod, 0, 0, s_zero, s_zero, p, fg)
        new_f.append(sf)
        new_b.append(sb)
        xs, _, _ = _trunk_layer(xs, mod, 1, 1, state_fwd[:, l], state_bwd[:, l], p, fg)
    return (xp, xs, jnp.stack(new_f, axis=1), jnp.stack(new_b, axis=1))
```

```python
import functools

import jax
import jax.numpy as jnp
from jax import lax
from jax.experimental import pallas as pl
from jax.experimental.pallas import tpu as pltpu

F32 = jnp.float32
BF16 = jnp.bfloat16

D_MODEL = 1024
W_A = D_MODEL // 2
DK = 128
DV = 128
H_A = W_A // DV
W_B = D_MODEL - W_A
WG = 128
G_B = W_B // WG
SGU_CHUNK = 128
CONV_K = 5
DN_CHUNK = 64
D_FF = 2816
N_MOD = 6
EPS = 1e-6

MOD_ROWS = 16
GATE_W = 128
INV_BASE = 16
FF_CHUNK = 256
P1_BATCH = 8
VMEM_LIMIT = 56 * 1024 * 1024


def _sigmoid(x):
    return 0.5 + 0.5 * jnp.tanh(0.5 * x)


def _silu(x):
    return x * _sigmoid(x)


def _gelu_tanh(x):
    c = 0.7978845608028654
    return 0.5 * x * (1.0 + jnp.tanh(c * (x + 0.044715 * (x * x * x))))


def _softplus(x):
    return jnp.maximum(x, 0.0) + jnp.log(1.0 + jnp.exp(-jnp.abs(x)))


def _dot(a, b):
    return jnp.dot(a, b, preferred_element_type=F32)


def _dot_nt(a, b):
    return lax.dot_general(a, b, (((1,), (1,)), ((), ())), preferred_element_type=F32)


def _split3(x):
    hi = x.astype(BF16)
    r1 = x - hi.astype(F32)
    mid = r1.astype(BF16)
    lo = (r1 - mid.astype(F32)).astype(BF16)
    return hi, mid, lo


def _dot_exact_lhs(a_bf16, x):
    hi, mid, lo = _split3(x)
    return _dot(a_bf16, hi) + _dot(a_bf16, mid) + _dot(a_bf16, lo)


def _ada_kernel(c_ref, w_ref, b_ref, o_ref):
    s = _silu(c_ref[...])
    o_ref[...] = _dot(s.astype(BF16), w_ref[...].astype(BF16)) + b_ref[...]


def _ada(cc, w_ada, b_ada):
    n_out = w_ada.shape[1]
    bn = D_MODEL
    return pl.pallas_call(
        _ada_kernel,
        grid=(n_out // bn,),
        in_specs=[
            pl.BlockSpec((MOD_ROWS, D_MODEL), lambda j: (0, 0)),
            pl.BlockSpec((D_MODEL, bn), lambda j: (0, j)),
            pl.BlockSpec((1, bn), lambda j: (0, j)),
        ],
        out_specs=pl.BlockSpec((MOD_ROWS, bn), lambda j: (0, j)),
        out_shape=jax.ShapeDtypeStruct((MOD_ROWS, n_out), F32),
        compiler_params=pltpu.CompilerParams(vmem_limit_bytes=VMEM_LIMIT),
        name="ada",
    )(cc, w_ada, b_ada.reshape(1, n_out))


def _inproj_kernel(x_ref, mod_ref, g1_ref, wm_ref, wba_ref, arow_ref, dtrow_ref, lng_ref, lnb_ref, ws_ref, bs_ref,
                   qkv_ref, z_ref, ba_ref, yb_ref, *, tm):
    x = x_ref[...]
    ms = jnp.mean(x * x, axis=-1, keepdims=True)
    xn = x * lax.rsqrt(ms + EPS) * g1_ref[...]
    h = xn * (1.0 + mod_ref[1:2, :]) + mod_ref[0:1, :]
    hb = h.astype(BF16)
    qkv_ref[...] = _dot(hb, wm_ref[:, 0:3 * W_A])
    z_ref[...] = _dot(hb, wm_ref[:, 3 * W_A:4 * W_A])
    logit = _dot(hb, wba_ref[...])
    lane = lax.broadcasted_iota(jnp.int32, logit.shape, 1)
    g_all = -jnp.exp(arow_ref[...]) * _softplus(logit + dtrow_ref[...])
    ri = lax.broadcasted_iota(jnp.int32, (tm, tm), 0)
    ci = lax.broadcasted_iota(jnp.int32, (tm, tm), 1)
    same_chunk = (ri // DN_CHUNK) == (ci // DN_CHUNK)
    cum_f = _dot_exact_lhs((same_chunk & (ri >= ci)).astype(BF16), g_all)
    cum_b = _dot_exact_lhs((same_chunk & (ri <= ci)).astype(BF16), g_all)
    ba_ref[...] = jnp.where(lane < 2 * H_A, _sigmoid(logit), jnp.where(lane < 3 * H_A, cum_f, cum_b))
    u = _gelu_tanh(_dot(hb, wm_ref[:, 4 * W_A:4 * W_A + W_B]))
    vb = _gelu_tanh(_dot(hb, wm_ref[:, 4 * W_A + W_B:4 * W_A + 2 * W_B]))
    mu = jnp.mean(vb, axis=-1, keepdims=True)
    vc = vb - mu
    var = jnp.mean(vc * vc, axis=-1, keepdims=True)
    vh = (vc * lax.rsqrt(var + EPS) * lng_ref[...] + lnb_ref[...]).astype(BF16)
    for c in range(tm // SGU_CHUNK):
        r = slice(c * SGU_CHUNK, (c + 1) * SGU_CHUNK)
        for g in range(G_B):
            l = slice(g * WG, (g + 1) * WG)
            mixed = _dot(ws_ref[g], vh[r, l]) + bs_ref[:, g:g + 1]
            yb_ref[r, l] = (u[r, l] * mixed).astype(BF16)


def _inproj(x2d, mod, g1, w_main, w_ba, a_row, dt_row, ln_g, ln_b, w_s, b_s_t, *, seq, row0, row_stride, tm):
    n_tok = x2d.shape[0]
    tiles_per_seq = seq // tm

    def mod_idx(i):
        return (row0 + (i // tiles_per_seq) * row_stride, 0, 0)

    const2 = lambda i: (0, 0)
    tok = lambda i: (i, 0)
    return pl.pallas_call(
        functools.partial(_inproj_kernel, tm=tm),
        grid=(n_tok // tm,),
        in_specs=[
            pl.BlockSpec((tm, D_MODEL), tok),
            pl.BlockSpec((None, N_MOD, D_MODEL), mod_idx),
            pl.BlockSpec((1, D_MODEL), const2),
            pl.BlockSpec(w_main.shape, const2),
            pl.BlockSpec(w_ba.shape, const2),
            pl.BlockSpec((1, GATE_W), const2),
            pl.BlockSpec((1, GATE_W), const2),
            pl.BlockSpec((1, W_B), const2),
            pl.BlockSpec((1, W_B), const2),
            pl.BlockSpec(w_s.shape, lambda i: (0, 0, 0)),
            pl.BlockSpec(b_s_t.shape, const2),
        ],
        out_specs=[
            pl.BlockSpec((tm, 3 * W_A), tok),
            pl.BlockSpec((tm, W_A), tok),
            pl.BlockSpec((tm, GATE_W), tok),
            pl.BlockSpec((tm, W_B), tok),
        ],
        out_shape=[
            jax.ShapeDtypeStruct((n_tok, 3 * W_A), F32),
            jax.ShapeDtypeStruct((n_tok, W_A), F32),
            jax.ShapeDtypeStruct((n_tok, GATE_W), F32),
            jax.ShapeDtypeStruct((n_tok, W_B), BF16),
        ],
        compiler_params=pltpu.CompilerParams(vmem_limit_bytes=VMEM_LIMIT),
        name="inproj",
    )(x2d, mod, g1, w_main, w_ba, a_row, dt_row, ln_g, ln_b, w_s, b_s_t)


def _tri_inv_lower(mats):
    n = DN_CHUNK
    nblk = n // INV_BASE
    assert len(mats) % 2 == 0
    lane1 = lax.broadcasted_iota(jnp.int32, (INV_BASE, n), 1)
    lane2 = lax.broadcasted_iota(jnp.int32, (INV_BASE, 2 * n), 1)
    sub2 = lax.broadcasted_iota(jnp.int32, (INV_BASE, 2 * n), 0)
    blk_masks = [(lane1 // INV_BASE) == rb for rb in range(nblk)]

    def compress(a):
        out = jnp.where(blk_masks[0], a[0:INV_BASE, :], 0.0)
        for rb in range(1, nblk):
            out = out + jnp.where(blk_masks[rb], a[rb * INV_BASE:(rb + 1) * INV_BASE, :], 0.0)
        return out

    packed = [jnp.concatenate([compress(mats[2 * p]), compress(mats[2 * p + 1])], axis=1)
              for p in range(len(mats) // 2)]
    eye = ((lane2 % INV_BASE) == sub2).astype(F32)
    xs = [eye for _ in packed]
    grp = (lane2 // INV_BASE) * INV_BASE
    for j in range(INV_BASE - 1):
        cols = [jnp.take_along_axis(nc, grp + j, axis=1) for nc in packed]
        xs = [x - c * x[j:j + 1, :] for x, c in zip(xs, cols)]

    def expand(xc):
        return jnp.concatenate([jnp.where(m, xc, 0.0) for m in blk_masks], axis=0)

    ds = []
    for x in xs:
        ds.append(expand(x[:, 0:n]))
        ds.append(expand(x[:, n:2 * n]))
    row = lax.broadcasted_iota(jnp.int32, (n, n), 0)
    col = lax.broadcasted_iota(jnp.int32, (n, n), 1)
    bs = INV_BASE
    while bs < n:
        cmask = ((row // (2 * bs)) == (col // (2 * bs))) & (((row // bs) % 2) == 1) & (((col // bs) % 2) == 0)
        cs = [jnp.where(cmask, a, 0.0).astype(BF16) for a in mats]
        dbs = [d.astype(BF16) for d in ds]
        t1 = [_dot(db, c) for db, c in zip(dbs, cs)]
        t2 = [_dot(t.astype(BF16), db) for t, db in zip(t1, dbs)]
        ds = [d - t for d, t in zip(ds, t2)]
        bs *= 2
    return ds


def _dn_kernel(q_ref, k_ref, v_ref, z_ref, ba_ref, cwq_ref, cwk_ref, cwv_ref,
               ng_ref, sf0_ref, sb0_ref,
               ya_ref, sf_ref, sb_ref,
               pad_ref, qs_ref, ks_ref, vs_ref, beta_ref, g_ref,
               r_ref, mq_ref, et_ref, o_ref, *, seq):
    t = seq
    c_len = DN_CHUNK
    n = t // c_len
    h = pl.program_id(1)
    rb = min(t, 256)

    def pick(x, base):
        out = x[:, base:base + 1]
        for hh in range(1, H_A):
            out = jnp.where(h == hh, x[:, base + hh:base + hh + 1], out)
        return out

    for r in range(0, t, rb):
        gates = ba_ref[r:r + rb, :]
        for d in range(2):
            beta_ref[d, r:r + rb, :] = pick(gates, d * H_A)
            g_ref[d, r:r + rb, :] = pick(gates, 2 * H_A + d * H_A)

    zeros8 = jnp.zeros((8, DK), F32)
    pad_ref[0:8, :] = zeros8
    pad_ref[8 + t:16 + t, :] = zeros8

    def conv_silu(x_ref, cw_ref, dst_ref, l2, scale):
        pad_ref[8:8 + t, :] = x_ref[...]
        for r in range(0, t, rb):
            acc = cw_ref[0:1, :] * pad_ref[6 + r:6 + r + rb, :]
            for j in range(1, CONV_K):
                acc = acc + cw_ref[j:j + 1, :] * pad_ref[6 + j + r:6 + j + r + rb, :]
            y = _silu(acc)
            if l2:
                y = y * (lax.rsqrt(jnp.sum(y * y, axis=-1, keepdims=True) + EPS) * scale)
            dst_ref[r:r + rb, :] = y

    conv_silu(q_ref, cwq_ref, qs_ref, True, DK ** -0.5)
    conv_silu(k_ref, cwk_ref, ks_ref, True, 1.0)
    conv_silu(v_ref, cwv_ref, vs_ref, False, 1.0)

    row = lax.broadcasted_iota(jnp.int32, (c_len, c_len), 0)
    col = lax.broadcasted_iota(jnp.int32, (c_len, c_len), 1)
    lower_incl = row >= col
    upper_incl = row <= col
    lower_strict = row > col
    nb = min(P1_BATCH, n)
    assert n % nb == 0

    def phase1(it, carry):
        cs = [it * nb + i for i in range(nb)]
        r64 = [pl.multiple_of(c * c_len, c_len) for c in cs]
        r128 = [pl.multiple_of(c * DK, DK) for c in cs]
        r192 = [pl.multiple_of(c * (DK + c_len), DK + c_len) for c in cs]
        r8 = [pl.multiple_of(c * 8, 8) for c in cs]
        items = [(i, d) for i in range(nb) for d in range(2)]
        q = [qs_ref[pl.ds(r, c_len), :] for r in r64]
        k = [ks_ref[pl.ds(r, c_len), :] for r in r64]
        v = [vs_ref[pl.ds(r, c_len), :] for r in r64]
        kb = [x.astype(BF16) for x in k]
        kk = [_dot_nt(x, x) for x in kb]
        qk = [_dot_nt(x.astype(BF16), y) for x, y in zip(q, kb)]
        beta = [beta_ref[d, pl.ds(r64[i], c_len), :] for i, d in items]
        cum = [jnp.broadcast_to(g_ref[d, pl.ds(r64[i], c_len), :], (c_len, DK)) for i, d in items]
        cum_t = [x[:, 0:c_len].T for x in cum]
        dm = [x[:, 0:c_len] - y for x, y in zip(cum, cum_t)]
        dec_sys = [jnp.where(lower_incl, jnp.exp(x if d == 0 else -x), 0.0) for (i, d), x in zip(items, dm)]
        dec_qk = [ds if d == 0 else jnp.where(upper_incl, jnp.exp(x), 0.0)
                  for (i, d), x, ds in zip(items, dm, dec_sys)]
        a = [jnp.where(lower_strict, b * kk[i] * ds, 0.0) for (i, d), b, ds in zip(items, beta, dec_sys)]
        tm = [x.astype(BF16) for x in _tri_inv_lower(a)]
        e_col = [jnp.exp(x) for x in cum]
        rhs = []
        for (i, d), b, e in zip(items, beta, e_col):
            if d == 0:
                rhs.append(jnp.concatenate([v[i] * b, k[i] * (b * e)], axis=1).astype(BF16))
            else:
                rhs.append(jnp.concatenate([v[i], k[i] * e], axis=1).astype(BF16))
        uw = []
        for (i, d), t_, r_, b in zip(items, tm, rhs, beta):
            if d == 0:
                uw.append(_dot(t_, r_))
            else:
                uw.append(b * lax.dot_general(t_, r_, (((0,), (0,)), ((), ())), preferred_element_type=F32))
        tot = [x[c_len - 1:c_len, :] if d == 0 else x[0:1, :] for (i, d), x in zip(items, cum)]
        k_dec_t = [(k[i] * jnp.exp(t_ - x)).T.astype(BF16) for (i, d), x, t_ in zip(items, cum, tot)]
        qkd = [(qk[i] * dq).astype(BF16) for (i, d), dq in zip(items, dec_qk)]
        uwb = [x.astype(BF16) for x in uw]
        kuw = [_dot(kt, x) for kt, x in zip(k_dec_t, uwb)]
        quw = [_dot(qd, x) for qd, x in zip(qkd, uwb)]
        for idx, (i, d) in enumerate(items):
            r_ref[d, pl.ds(r128[i], DK), :] = kuw[idx][:, 0:DV]
            mq_ref[d, pl.ds(r192[i], DK), :] = (-kuw[idx][:, DV:DV + DK]).astype(BF16)
            mq_ref[d, pl.ds(r192[i] + DK, c_len), :] = (q[i] * e_col[idx] - quw[idx][:, DV:DV + DK]).astype(BF16)
            o_ref[d, pl.ds(r64[i], c_len), :] = quw[idx][:, 0:DV]
            et_ref[d, pl.ds(r8[i], 8), :] = jnp.exp(jnp.broadcast_to(tot[idx], (8, DK)))
        return carry

    lax.fori_loop(0, n // nb, phase1, 0)

    def phase2(i, carry):
        new = []
        for d in range(2):
            s = carry[d]
            c = i if d == 0 else n - 1 - i
            r64 = pl.multiple_of(c * c_len, c_len)
            r128 = pl.multiple_of(c * DK, DK)
            r192 = pl.multiple_of(c * (DK + c_len), DK + c_len)
            ms = _dot(mq_ref[d, pl.ds(r192, DK + c_len), :], s.astype(BF16))
            et = et_ref[d, pl.ds(pl.multiple_of(c * 8, 8), 8), :]
            new.append(s * et[0:1, :] + r_ref[d, pl.ds(r128, DK), :] + ms[0:DK])
            o_ref[d, pl.ds(r64, c_len), :] = o_ref[d, pl.ds(r64, c_len), :] + ms[DK:DK + c_len]
        return tuple(new)

    s_f, s_b = lax.fori_loop(0, n, phase2, (sf0_ref[...], sb0_ref[...]))
    sf_ref[...] = s_f
    sb_ref[...] = s_b

    for r in range(0, t, rb):
        o = o_ref[0, r:r + rb, :] + o_ref[1, r:r + rb, :]
        ms = jnp.mean(o * o, axis=-1, keepdims=True)
        y = o * lax.rsqrt(ms + EPS) * ng_ref[...]
        ya_ref[r:r + rb, :] = (y * _silu(z_ref[r:r + rb, :])).astype(BF16)


def _deltanet(qkv, z, ba, conv_w, norm_g, s_f0, s_b0):
    bsz, t, _ = qkv.shape
    n = t // DN_CHUNK
    col = lambda off: (lambda b, h: (b, 0, off + h))
    const2 = lambda b, h: (0, 0)
    cw = lambda off: (lambda b, h: (0, off + h))
    state = lambda b, h: (b, h, 0, 0)
    return pl.pallas_call(
        functools.partial(_dn_kernel, seq=t),
        grid=(bsz, H_A),
        in_specs=[
            pl.BlockSpec((None, t, DK), col(0)),
            pl.BlockSpec((None, t, DK), col(H_A)),
            pl.BlockSpec((None, t, DV), col(2 * H_A)),
            pl.BlockSpec((None, t, DV), col(0)),
            pl.BlockSpec((None, t, GATE_W), lambda b, h: (b, 0, 0)),
            pl.BlockSpec((CONV_K, DK), cw(0)),
            pl.BlockSpec((CONV_K, DK), cw(H_A)),
            pl.BlockSpec((CONV_K, DV), cw(2 * H_A)),
            pl.BlockSpec((1, DV), const2),
            pl.BlockSpec((None, None, DK, DV), state),
            pl.BlockSpec((None, None, DK, DV), state),
        ],
        out_specs=[
            pl.BlockSpec((None, t, DV), col(0)),
            pl.BlockSpec((None, None, DK, DV), state),
            pl.BlockSpec((None, None, DK, DV), state),
        ],
        out_shape=[
            jax.ShapeDtypeStruct((bsz, t, W_A), BF16),
            jax.ShapeDtypeStruct((bsz, H_A, DK, DV), F32),
            jax.ShapeDtypeStruct((bsz, H_A, DK, DV), F32),
        ],
        scratch_shapes=[
            pltpu.VMEM((t + 16, DK), F32),
            pltpu.VMEM((t, DK), F32),
            pltpu.VMEM((t, DK), F32),
            pltpu.VMEM((t, DV), F32),
            pltpu.VMEM((2, t, 1), F32),
            pltpu.VMEM((2, t, 1), F32),
            pltpu.VMEM((2, n * DK, DV), F32),
            pltpu.VMEM((2, n * (DK + DN_CHUNK), DK), BF16),
            pltpu.VMEM((2, n * 8, DV), F32),
            pltpu.VMEM((2, t, DV), F32),
        ],
        compiler_params=pltpu.CompilerParams(vmem_limit_bytes=VMEM_LIMIT),
        name="deltanet",
    )(qkv, qkv, qkv, z, ba, conv_w, conv_w, conv_w, norm_g, s_f0, s_b0)


def _outffn_kernel(x_ref, ya_ref, yb_ref, mod_ref, n2_ref, fg_ref, wo_ref, wg_ref, wu_ref, wd_ref, o_ref):
    y = _dot(ya_ref[...], wo_ref[0:W_A, :]) + _dot(yb_ref[...], wo_ref[W_A:D_MODEL, :])
    x1 = x_ref[...] + mod_ref[2:3, :] * y
    ms = jnp.mean(x1 * x1, axis=-1, keepdims=True)
    h = (x1 * lax.rsqrt(ms + EPS) * n2_ref[...]) * (1.0 + mod_ref[4:5, :]) + mod_ref[3:4, :]
    hb = h.astype(BF16)
    acc = None
    for c in range(D_FF // FF_CHUNK):
        l = slice(c * FF_CHUNK, (c + 1) * FF_CHUNK)
        act = _silu(_dot(hb, wg_ref[:, l])) * _dot(hb, wu_ref[:, l])
        part = _dot(act.astype(BF16), wd_ref[l, :])
        acc = part if acc is None else acc + part
    x2 = x1 + mod_ref[5:6, :] * acc
    ms2 = jnp.mean(x2 * x2, axis=-1, keepdims=True)
    o_ref[...] = x2 * lax.rsqrt(ms2 + EPS) * fg_ref[...]


def _outffn(x2d, ya, yb, mod, n2, fg, w_out, w_gate, w_up, w_down, *, seq, row0, row_stride, tm):
    n_tok = x2d.shape[0]
    tiles_per_seq = seq // tm

    def mod_idx(i):
        return (row0 + (i // tiles_per_seq) * row_stride, 0, 0)

    const2 = lambda i: (0, 0)
    tok = lambda i: (i, 0)
    resident = lambda w: pl.BlockSpec(w.shape, const2, pipeline_mode=pl.Buffered(1))
    return pl.pallas_call(
        _outffn_kernel,
        grid=(n_tok // tm,),
        in_specs=[
            pl.BlockSpec((tm, D_MODEL), tok),
            pl.BlockSpec((tm, W_A), tok),
            pl.BlockSpec((tm, W_B), tok),
            pl.BlockSpec((None, N_MOD, D_MODEL), mod_idx),
            pl.BlockSpec((1, D_MODEL), const2),
            pl.BlockSpec((1, D_MODEL), const2),
            resident(w_out), resident(w_gate), resident(w_up), resident(w_down),
        ],
        out_specs=pl.BlockSpec((tm, D_MODEL), tok),
        out_shape=jax.ShapeDtypeStruct((n_tok, D_MODEL), F32),
        compiler_params=pltpu.CompilerParams(vmem_limit_bytes=VMEM_LIMIT),
        name="outffn",
    )(x2d, ya, yb, mod, n2, fg, w_out, w_gate, w_up, w_down)


def _trunk_layer(x, mod, row0, row_stride, s_f0, s_b0, p, final_g):
    bsz, t, _ = x.shape
    x2d = x.reshape(bsz * t, D_MODEL)
    tm = 256
    qkv, z, ba, yb = _inproj(x2d, mod, p["g1"], p["w_main"], p["w_ba"], p["a_row"], p["dt_row"], p["ln_g"], p["ln_b"], p["w_s"],
                             p["b_s_t"], seq=t, row0=row0, row_stride=row_stride, tm=tm)
    ya, s_f, s_b = _deltanet(qkv.reshape(bsz, t, 3 * W_A), z.reshape(bsz, t, W_A), ba.reshape(bsz, t, GATE_W),
                             p["conv_w"], p["dn_g"], s_f0, s_b0)
    out = _outffn(x2d, ya.reshape(bsz * t, W_A), yb, mod, p["g2"], final_g, p["w_out"], p["w_gate"], p["w_up"],
                  p["w_down"], seq=t, row0=row0, row_stride=row_stride, tm=tm)
    return out.reshape(bsz, t, D_MODEL), s_f, s_b


def kernel(x_prompt, x_sample, state_fwd, state_bwd, c, c_ctx, w_ada, b_ada, norm1_g, norm2_g, w_in, conv_w,
           a_log, dt_bias, dn_norm_g, sgu_ln_g, sgu_ln_b, sgu_w, sgu_b, w_out, w_gate, w_up, w_down, final_g):
    depth = w_ada.shape[0]
    assert depth == 1, "the final RMSNorm is fused into the last layer; only depth 1 is supported"
    n_lat = c.shape[0]
    assert 1 + n_lat <= MOD_ROWS
    cc = jnp.zeros((MOD_ROWS, D_MODEL), F32).at[0].set(c_ctx).at[1:1 + n_lat].set(c)
    s_zero = jnp.zeros((x_prompt.shape[0], H_A, DK, DV), F32)
    xp, xs = x_prompt, x_sample
    new_f, new_b = [], []
    gate_pad = jnp.zeros((2 * H_A,), F32)
    for l in range(depth):
        w_in_b = w_in[l].astype(BF16)
        n_main = 4 * W_A + 2 * W_B
        p = dict(
            g1=norm1_g[l].reshape(1, D_MODEL), g2=norm2_g[l].reshape(1, D_MODEL),
            w_main=w_in_b[:, :n_main],
            w_ba=jnp.pad(w_in_b[:, n_main:], ((0, 0), (0, GATE_W - 4 * H_A))),
            ln_g=sgu_ln_g[l].reshape(1, W_B), ln_b=sgu_ln_b[l].reshape(1, W_B),
            w_s=sgu_w[l].astype(BF16), b_s_t=jnp.transpose(sgu_b[l]),
            conv_w=conv_w[l],
            a_row=jnp.pad(jnp.concatenate([gate_pad, a_log[l].reshape(-1)]), (0, GATE_W - 4 * H_A)).reshape(1, GATE_W),
            dt_row=jnp.pad(jnp.concatenate([gate_pad, dt_bias[l].reshape(-1)]), (0, GATE_W - 4 * H_A)).reshape(1, GATE_W),
            dn_g=dn_norm_g[l].reshape(1, DV),
            w_out=w_out[l].astype(BF16), w_gate=w_gate[l].astype(BF16), w_up=w_up[l].astype(BF16),
            w_down=w_down[l].astype(BF16),
        )
        mod = _ada(cc, w_ada[l], b_ada[l]).reshape(MOD_ROWS, N_MOD, D_MODEL)
        fg = final_g.reshape(1, D_MODEL)
        xp, sf, sb = _trunk_layer(xp, mod, 0, 0, s_zero, s_zero, p, fg)
        new_f.append(sf)
        new_b.append(sb)
        xs, _, _ = _trunk_layer(xs, mod, 1, 1, state_fwd[:, l], state_bwd[:, l], p, fg)
    return (xp, xs, jnp.stack(new_f, axis=1), jnp.stack(new_b, axis=1))
```

```python
import functools

import jax
import jax.numpy as jnp
from jax import lax
from jax.experimental import pallas as pl
from jax.experimental.pallas import tpu as pltpu

F32 = jnp.float32
BF16 = jnp.bfloat16

D_MODEL = 1024
W_A = D_MODEL // 2
DK = 128
DV = 128
H_A = W_A // DV
W_B = D_MODEL - W_A
WG = 128
G_B = W_B // WG
SGU_CHUNK = 128
CONV_K = 5
DN_CHUNK = 64
D_FF = 2816
N_MOD = 6
EPS = 1e-6

MOD_ROWS = 16
GATE_W = 128
INV_BASE = 16
FF_CHUNK = 256
IN_SUB = 256
IN_TM = 512
FFN_SUB = 256
FFN_TM = 512
P1_BATCH = 8
VMEM_LIMIT = 56 * 1024 * 1024


def _sigmoid(x):
    return 0.5 + 0.5 * jnp.tanh(0.5 * x)


def _silu(x):
    return x * _sigmoid(x)


def _gelu_tanh(x):
    c = 0.7978845608028654
    return 0.5 * x * (1.0 + jnp.tanh(c * (x + 0.044715 * (x * x * x))))


def _softplus(x):
    return jnp.maximum(x, 0.0) + jnp.log(1.0 + jnp.exp(-jnp.abs(x)))


def _dot(a, b):
    return jnp.dot(a, b, preferred_element_type=F32)


def _dot_nt(a, b):
    return lax.dot_general(a, b, (((1,), (1,)), ((), ())), preferred_element_type=F32)


def _split3(x):
    hi = x.astype(BF16)
    r1 = x - hi.astype(F32)
    mid = r1.astype(BF16)
    lo = (r1 - mid.astype(F32)).astype(BF16)
    return hi, mid, lo


def _dot_exact_lhs(a_bf16, x):
    hi, mid, lo = _split3(x)
    return _dot(a_bf16, hi) + _dot(a_bf16, mid) + _dot(a_bf16, lo)


def _ada_kernel(c_ref, w_ref, b_ref, o_ref):
    s = _silu(c_ref[...])
    o_ref[...] = _dot(s.astype(BF16), w_ref[...].astype(BF16)) + b_ref[...]


def _ada(cc, w_ada, b_ada):
    n_out = w_ada.shape[1]
    bn = D_MODEL
    return pl.pallas_call(
        _ada_kernel,
        grid=(n_out // bn,),
        in_specs=[
            pl.BlockSpec((MOD_ROWS, D_MODEL), lambda j: (0, 0)),
            pl.BlockSpec((D_MODEL, bn), lambda j: (0, j)),
            pl.BlockSpec((1, bn), lambda j: (0, j)),
        ],
        out_specs=pl.BlockSpec((MOD_ROWS, bn), lambda j: (0, j)),
        out_shape=jax.ShapeDtypeStruct((MOD_ROWS, n_out), F32),
        compiler_params=pltpu.CompilerParams(vmem_limit_bytes=VMEM_LIMIT),
        name="ada",
    )(cc, w_ada, b_ada.reshape(1, n_out))


def _inproj_kernel(x_ref, mod_ref, g1_ref, wm_ref, wba_ref, arow_ref, dtrow_ref, lng_ref, lnb_ref, ws_ref, bs_ref,
                   qkv_ref, z_ref, ba_ref, yb_ref, *, tm):
    subs = [slice(r, r + IN_SUB) for r in range(0, tm, IN_SUB)]
    ri = lax.broadcasted_iota(jnp.int32, (IN_SUB, IN_SUB), 0)
    ci = lax.broadcasted_iota(jnp.int32, (IN_SUB, IN_SUB), 1)
    same_chunk = (ri // DN_CHUNK) == (ci // DN_CHUNK)
    prefix_op = (same_chunk & (ri >= ci)).astype(BF16)
    suffix_op = (same_chunk & (ri <= ci)).astype(BF16)

    def norm(r):
        x = x_ref[r, :]
        ms = jnp.mean(x * x, axis=-1, keepdims=True)
        xn = x * lax.rsqrt(ms + EPS) * g1_ref[...]
        return (xn * (1.0 + mod_ref[1:2, :]) + mod_ref[0:1, :]).astype(BF16)

    def matmul_tasks(r, hb):
        def qkv_piece(j):
            def run():
                l = slice(j * W_A, (j + 1) * W_A)
                qkv_ref[r, l] = _dot(hb, wm_ref[:, l]).astype(BF16)
            return run

        def z_piece():
            z_ref[r, :] = _dot(hb, wm_ref[:, 3 * W_A:4 * W_A]).astype(BF16)

        def gate_piece():
            logit = _dot(hb, wba_ref[...])
            lane = lax.broadcasted_iota(jnp.int32, logit.shape, 1)
            g_all = -jnp.exp(arow_ref[...]) * _softplus(logit + dtrow_ref[...])
            cum_f = _dot_exact_lhs(prefix_op, g_all)
            cum_b = _dot_exact_lhs(suffix_op, g_all)
            ba_ref[r, :] = jnp.where(lane < 2 * H_A, _sigmoid(logit), jnp.where(lane < 3 * H_A, cum_f, cum_b))

        return [qkv_piece(0), qkv_piece(1), qkv_piece(2), z_piece, gate_piece]

    def sgu_tasks(r, st):
        def gelu_u():
            st["u"] = _gelu_tanh(st["u"])

        def gelu_v():
            st["v"] = _gelu_tanh(st["v"])

        def layernorm_v():
            vb = st["v"]
            mu = jnp.mean(vb, axis=-1, keepdims=True)
            vc = vb - mu
            var = jnp.mean(vc * vc, axis=-1, keepdims=True)
            st["v"] = (vc * lax.rsqrt(var + EPS) * lng_ref[...] + lnb_ref[...]).astype(BF16)

        def mix(c):
            def run():
                rr = slice(c * SGU_CHUNK, (c + 1) * SGU_CHUNK)
                ro = slice(r.start + c * SGU_CHUNK, r.start + (c + 1) * SGU_CHUNK)
                for g in range(G_B):
                    l = slice(g * WG, (g + 1) * WG)
                    mixed = _dot(ws_ref[g], st["v"][rr, l]) + bs_ref[:, g:g + 1]
                    yb_ref[ro, l] = (st["u"][rr, l] * mixed).astype(BF16)
            return run

        return [gelu_u, gelu_v, layernorm_v] + [mix(c) for c in range(IN_SUB // SGU_CHUNK)]

    hb = norm(subs[0])
    for i, r in enumerate(subs):
        st = {"u": _dot(hb, wm_ref[:, 4 * W_A:4 * W_A + W_B]),
              "v": _dot(hb, wm_ref[:, 4 * W_A + W_B:4 * W_A + 2 * W_B])}
        mm = matmul_tasks(r, hb)
        ew = sgu_tasks(r, st)
        nxt = {}
        if i + 1 < len(subs):
            ew.append(lambda i=i: nxt.update(hb=norm(subs[i + 1])))
        for k in range(max(len(mm), len(ew))):
            if k < len(mm):
                mm[k]()
            if k < len(ew):
                ew[k]()
        hb = nxt.get("hb")


def _inproj(x2d, mod, g1, w_main, w_ba, a_row, dt_row, ln_g, ln_b, w_s, b_s_t, *, seq, row0, row_stride, tm):
    n_tok = x2d.shape[0]
    assert row_stride == 0 or seq % tm == 0

    def mod_idx(i):
        return (row0 + ((i * tm) // seq) * row_stride, 0, 0)

    const2 = lambda i: (0, 0)
    tok = lambda i: (i, 0)
    return pl.pallas_call(
        functools.partial(_inproj_kernel, tm=tm),
        grid=(n_tok // tm,),
        in_specs=[
            pl.BlockSpec((tm, D_MODEL), tok),
            pl.BlockSpec((None, N_MOD, D_MODEL), mod_idx),
            pl.BlockSpec((1, D_MODEL), const2),
            pl.BlockSpec(w_main.shape, const2, pipeline_mode=pl.Buffered(1)),
            pl.BlockSpec(w_ba.shape, const2),
            pl.BlockSpec((1, GATE_W), const2),
            pl.BlockSpec((1, GATE_W), const2),
            pl.BlockSpec((1, W_B), const2),
            pl.BlockSpec((1, W_B), const2),
            pl.BlockSpec(w_s.shape, lambda i: (0, 0, 0)),
            pl.BlockSpec(b_s_t.shape, const2),
        ],
        out_specs=[
            pl.BlockSpec((tm, 3 * W_A), tok),
            pl.BlockSpec((tm, W_A), tok),
            pl.BlockSpec((tm, GATE_W), tok),
            pl.BlockSpec((tm, W_B), tok),
        ],
        out_shape=[
            jax.ShapeDtypeStruct((n_tok, 3 * W_A), BF16),
            jax.ShapeDtypeStruct((n_tok, W_A), BF16),
            jax.ShapeDtypeStruct((n_tok, GATE_W), F32),
            jax.ShapeDtypeStruct((n_tok, W_B), BF16),
        ],
        compiler_params=pltpu.CompilerParams(vmem_limit_bytes=VMEM_LIMIT),
        name="inproj",
    )(x2d, mod, g1, w_main, w_ba, a_row, dt_row, ln_g, ln_b, w_s, b_s_t)


def _tri_inv_lower(mats):
    n = DN_CHUNK
    nblk = n // INV_BASE
    assert len(mats) % 2 == 0
    lane1 = lax.broadcasted_iota(jnp.int32, (INV_BASE, n), 1)
    lane2 = lax.broadcasted_iota(jnp.int32, (INV_BASE, 2 * n), 1)
    sub2 = lax.broadcasted_iota(jnp.int32, (INV_BASE, 2 * n), 0)
    blk_masks = [(lane1 // INV_BASE) == rb for rb in range(nblk)]

    def compress(a):
        out = jnp.where(blk_masks[0], a[0:INV_BASE, :], 0.0)
        for rb in range(1, nblk):
            out = out + jnp.where(blk_masks[rb], a[rb * INV_BASE:(rb + 1) * INV_BASE, :], 0.0)
        return out

    packed = [jnp.concatenate([compress(mats[2 * p]), compress(mats[2 * p + 1])], axis=1)
              for p in range(len(mats) // 2)]
    eye = ((lane2 % INV_BASE) == sub2).astype(F32)
    xs = [eye for _ in packed]
    grp = (lane2 // INV_BASE) * INV_BASE
    for j in range(INV_BASE - 1):
        cols = [jnp.take_along_axis(nc, grp + j, axis=1) for nc in packed]
        xs = [x - c * x[j:j + 1, :] for x, c in zip(xs, cols)]

    def expand(xc):
        return jnp.concatenate([jnp.where(m, xc, 0.0) for m in blk_masks], axis=0)

    ds = []
    for x in xs:
        ds.append(expand(x[:, 0:n]))
        ds.append(expand(x[:, n:2 * n]))
    row = lax.broadcasted_iota(jnp.int32, (n, n), 0)
    col = lax.broadcasted_iota(jnp.int32, (n, n), 1)
    bs = INV_BASE
    while bs < n:
        cmask = ((row // (2 * bs)) == (col // (2 * bs))) & (((row // bs) % 2) == 1) & (((col // bs) % 2) == 0)
        cs = [jnp.where(cmask, a, 0.0).astype(BF16) for a in mats]
        dbs = [d.astype(BF16) for d in ds]
        t1 = [_dot(db, c) for db, c in zip(dbs, cs)]
        t2 = [_dot(t.astype(BF16), db) for t, db in zip(t1, dbs)]
        ds = [d - t for d, t in zip(ds, t2)]
        bs *= 2
    return ds


def _dn_kernel(q_ref, k_ref, v_ref, z_ref, ba_ref, cwq_ref, cwk_ref, cwv_ref,
               ng_ref, sf0_ref, sb0_ref,
               ya_ref, sf_ref, sb_ref,
               pad_ref, qs_ref, ks_ref, vs_ref, beta_ref, g_ref,
               r_ref, mq_ref, et_ref, o_ref, *, seq):
    t = seq
    c_len = DN_CHUNK
    n = t // c_len
    h = pl.program_id(1)
    rb = min(t, 256)

    def pick(x, base):
        out = x[:, base:base + 1]
        for hh in range(1, H_A):
            out = jnp.where(h == hh, x[:, base + hh:base + hh + 1], out)
        return out

    for r in range(0, t, rb):
        gates = ba_ref[r:r + rb, :]
        for d in range(2):
            beta_ref[d, r:r + rb, :] = pick(gates, d * H_A)
            g_ref[d, r:r + rb, :] = pick(gates, 2 * H_A + d * H_A)

    zeros8 = jnp.zeros((8, DK), F32)
    pad_ref[0:8, :] = zeros8
    pad_ref[8 + t:16 + t, :] = zeros8

    def conv_silu(x_ref, cw_ref, dst_ref, l2, scale):
        pad_ref[8:8 + t, :] = x_ref[...].astype(F32)
        for r in range(0, t, rb):
            acc = cw_ref[0:1, :] * pad_ref[6 + r:6 + r + rb, :]
            for j in range(1, CONV_K):
                acc = acc + cw_ref[j:j + 1, :] * pad_ref[6 + j + r:6 + j + r + rb, :]
            y = _silu(acc)
            if l2:
                y = y * (lax.rsqrt(jnp.sum(y * y, axis=-1, keepdims=True) + EPS) * scale)
            dst_ref[r:r + rb, :] = y

    conv_silu(q_ref, cwq_ref, qs_ref, True, DK ** -0.5)
    conv_silu(k_ref, cwk_ref, ks_ref, True, 1.0)
    conv_silu(v_ref, cwv_ref, vs_ref, False, 1.0)

    row = lax.broadcasted_iota(jnp.int32, (c_len, c_len), 0)
    col = lax.broadcasted_iota(jnp.int32, (c_len, c_len), 1)
    lower_incl = row >= col
    upper_incl = row <= col
    lower_strict = row > col
    nb = min(P1_BATCH, n)
    assert n % nb == 0

    def phase1(it, carry):
        cs = [it * nb + i for i in range(nb)]
        r64 = [pl.multiple_of(c * c_len, c_len) for c in cs]
        r128 = [pl.multiple_of(c * DK, DK) for c in cs]
        r192 = [pl.multiple_of(c * (DK + c_len), DK + c_len) for c in cs]
        r8 = [pl.multiple_of(c * 8, 8) for c in cs]
        items = [(i, d) for i in range(nb) for d in range(2)]
        q = [qs_ref[pl.ds(r, c_len), :] for r in r64]
        k = [ks_ref[pl.ds(r, c_len), :] for r in r64]
        v = [vs_ref[pl.ds(r, c_len), :] for r in r64]
        kb = [x.astype(BF16) for x in k]
        kk = [_dot_nt(x, x) for x in kb]
        qk = [_dot_nt(x.astype(BF16), y) for x, y in zip(q, kb)]
        beta = [beta_ref[d, pl.ds(r64[i], c_len), :] for i, d in items]
        cum = [jnp.broadcast_to(g_ref[d, pl.ds(r64[i], c_len), :], (c_len, DK)) for i, d in items]
        cum_t = [x[:, 0:c_len].T for x in cum]
        dm = [x[:, 0:c_len] - y for x, y in zip(cum, cum_t)]
        dec_sys = [jnp.where(lower_incl, jnp.exp(x if d == 0 else -x), 0.0) for (i, d), x in zip(items, dm)]
        dec_qk = [ds if d == 0 else jnp.where(upper_incl, jnp.exp(x), 0.0)
                  for (i, d), x, ds in zip(items, dm, dec_sys)]
        a = [jnp.where(lower_strict, b * kk[i] * ds, 0.0) for (i, d), b, ds in zip(items, beta, dec_sys)]
        tm = [x.astype(BF16) for x in _tri_inv_lower(a)]
        e_col = [jnp.exp(x) for x in cum]
        rhs = []
        for (i, d), b, e in zip(items, beta, e_col):
            if d == 0:
                rhs.append(jnp.concatenate([v[i] * b, k[i] * (b * e)], axis=1).astype(BF16))
            else:
                rhs.append(jnp.concatenate([v[i], k[i] * e], axis=1).astype(BF16))
        uw = []
        for (i, d), t_, r_, b in zip(items, tm, rhs, beta):
            if d == 0:
                uw.append(_dot(t_, r_))
            else:
                uw.append(b * lax.dot_general(t_, r_, (((0,), (0,)), ((), ())), preferred_element_type=F32))
        tot = [x[c_len - 1:c_len, :] if d == 0 else x[0:1, :] for (i, d), x in zip(items, cum)]
        k_dec_t = [(k[i] * jnp.exp(t_ - x)).T.astype(BF16) for (i, d), x, t_ in zip(items, cum, tot)]
        qkd = [(qk[i] * dq).astype(BF16) for (i, d), dq in zip(items, dec_qk)]
        uwb = [x.astype(BF16) for x in uw]
        kuw = [_dot(kt, x) for kt, x in zip(k_dec_t, uwb)]
        quw = [_dot(qd, x) for qd, x in zip(qkd, uwb)]
        for idx, (i, d) in enumerate(items):
            r_ref[d, pl.ds(r128[i], DK), :] = kuw[idx][:, 0:DV]
            mq_ref[d, pl.ds(r192[i], DK), :] = (-kuw[idx][:, DV:DV + DK]).astype(BF16)
            mq_ref[d, pl.ds(r192[i] + DK, c_len), :] = (q[i] * e_col[idx] - quw[idx][:, DV:DV + DK]).astype(BF16)
            o_ref[d, pl.ds(r64[i], c_len), :] = quw[idx][:, 0:DV]
            et_ref[d, pl.ds(r8[i], 8), :] = jnp.exp(jnp.broadcast_to(tot[idx], (8, DK)))
        return carry

    lax.fori_loop(0, n // nb, phase1, 0)

    def phase2(i, carry):
        new = []
        for d in range(2):
            s = carry[d]
            c = i if d == 0 else n - 1 - i
            r64 = pl.multiple_of(c * c_len, c_len)
            r128 = pl.multiple_of(c * DK, DK)
            r192 = pl.multiple_of(c * (DK + c_len), DK + c_len)
            ms = _dot(mq_ref[d, pl.ds(r192, DK + c_len), :], s.astype(BF16))
            et = et_ref[d, pl.ds(pl.multiple_of(c * 8, 8), 8), :]
            new.append(s * et[0:1, :] + r_ref[d, pl.ds(r128, DK), :] + ms[0:DK])
            o_ref[d, pl.ds(r64, c_len), :] = o_ref[d, pl.ds(r64, c_len), :] + ms[DK:DK + c_len]
        return tuple(new)

    s_f, s_b = lax.fori_loop(0, n, phase2, (sf0_ref[...], sb0_ref[...]))
    sf_ref[...] = s_f
    sb_ref[...] = s_b

    for r in range(0, t, rb):
        o = o_ref[0, r:r + rb, :] + o_ref[1, r:r + rb, :]
        ms = jnp.mean(o * o, axis=-1, keepdims=True)
        y = o * lax.rsqrt(ms + EPS) * ng_ref[...]
        ya_ref[r:r + rb, :] = (y * _silu(z_ref[r:r + rb, :].astype(F32))).astype(BF16)


def _deltanet(qkv, z, ba, conv_w, norm_g, s_f0, s_b0):
    bsz, t, _ = qkv.shape
    n = t // DN_CHUNK
    col = lambda off: (lambda b, h: (b, 0, off + h))
    const2 = lambda b, h: (0, 0)
    cw = lambda off: (lambda b, h: (0, off + h))
    state = lambda b, h: (b, h, 0, 0)
    return pl.pallas_call(
        functools.partial(_dn_kernel, seq=t),
        grid=(bsz, H_A),
        in_specs=[
            pl.BlockSpec((None, t, DK), col(0)),
            pl.BlockSpec((None, t, DK), col(H_A)),
            pl.BlockSpec((None, t, DV), col(2 * H_A)),
            pl.BlockSpec((None, t, DV), col(0)),
            pl.BlockSpec((None, t, GATE_W), lambda b, h: (b, 0, 0)),
            pl.BlockSpec((CONV_K, DK), cw(0)),
            pl.BlockSpec((CONV_K, DK), cw(H_A)),
            pl.BlockSpec((CONV_K, DV), cw(2 * H_A)),
            pl.BlockSpec((1, DV), const2),
            pl.BlockSpec((None, None, DK, DV), state),
            pl.BlockSpec((None, None, DK, DV), state),
        ],
        out_specs=[
            pl.BlockSpec((None, t, DV), col(0)),
            pl.BlockSpec((None, None, DK, DV), state),
            pl.BlockSpec((None, None, DK, DV), state),
        ],
        out_shape=[
            jax.ShapeDtypeStruct((bsz, t, W_A), BF16),
            jax.ShapeDtypeStruct((bsz, H_A, DK, DV), F32),
            jax.ShapeDtypeStruct((bsz, H_A, DK, DV), F32),
        ],
        scratch_shapes=[
            pltpu.VMEM((t + 16, DK), F32),
            pltpu.VMEM((t, DK), F32),
            pltpu.VMEM((t, DK), F32),
            pltpu.VMEM((t, DV), F32),
            pltpu.VMEM((2, t, 1), F32),
            pltpu.VMEM((2, t, 1), F32),
            pltpu.VMEM((2, n * DK, DV), F32),
            pltpu.VMEM((2, n * (DK + DN_CHUNK), DK), BF16),
            pltpu.VMEM((2, n * 8, DV), F32),
            pltpu.VMEM((2, t, DV), F32),
        ],
        compiler_params=pltpu.CompilerParams(vmem_limit_bytes=VMEM_LIMIT),
        name="deltanet",
    )(qkv, qkv, qkv, z, ba, conv_w, conv_w, conv_w, norm_g, s_f0, s_b0)


def _outffn_kernel(x_ref, ya_ref, yb_ref, mod_ref, n2_ref, fg_ref, wo_ref, wg_ref, wu_ref, wd_ref, o_ref,
                   act_ref, *, tm):
    subs = [slice(r, r + FFN_SUB) for r in range(0, tm, FFN_SUB)]
    hb = []
    for r in subs:
        y = _dot(ya_ref[r, :], wo_ref[0:W_A, :]) + _dot(yb_ref[r, :], wo_ref[W_A:D_MODEL, :])
        x1 = x_ref[r, :] + mod_ref[2:3, :] * y
        o_ref[r, :] = x1
        ms = jnp.mean(x1 * x1, axis=-1, keepdims=True)
        h = (x1 * lax.rsqrt(ms + EPS) * n2_ref[...]) * (1.0 + mod_ref[4:5, :]) + mod_ref[3:4, :]
        hb.append(h.astype(BF16))
    for c in range(D_FF // FF_CHUNK):
        l = slice(c * FF_CHUNK, (c + 1) * FF_CHUNK)
        for r, h in zip(subs, hb):
            act_ref[r, l] = (_silu(_dot(h, wg_ref[:, l])) * _dot(h, wu_ref[:, l])).astype(BF16)
    for r in subs:
        x2 = o_ref[r, :] + mod_ref[5:6, :] * _dot(act_ref[r, :], wd_ref[...])
        ms2 = jnp.mean(x2 * x2, axis=-1, keepdims=True)
        o_ref[r, :] = x2 * lax.rsqrt(ms2 + EPS) * fg_ref[...]


def _outffn(x2d, ya, yb, mod, n2, fg, w_out, w_gate, w_up, w_down, *, seq, row0, row_stride, tm):
    n_tok = x2d.shape[0]
    assert row_stride == 0 or seq % tm == 0

    def mod_idx(i):
        return (row0 + ((i * tm) // seq) * row_stride, 0, 0)

    const2 = lambda i: (0, 0)
    tok = lambda i: (i, 0)
    resident = lambda w: pl.BlockSpec(w.shape, const2, pipeline_mode=pl.Buffered(1))
    return pl.pallas_call(
        functools.partial(_outffn_kernel, tm=tm),
        grid=(n_tok // tm,),
        in_specs=[
            pl.BlockSpec((tm, D_MODEL), tok),
            pl.BlockSpec((tm, W_A), tok),
            pl.BlockSpec((tm, W_B), tok),
            pl.BlockSpec((None, N_MOD, D_MODEL), mod_idx),
            pl.BlockSpec((1, D_MODEL), const2),
            pl.BlockSpec((1, D_MODEL), const2),
            resident(w_out), resident(w_gate), resident(w_up), resident(w_down),
        ],
        out_specs=pl.BlockSpec((tm, D_MODEL), tok),
        out_shape=jax.ShapeDtypeStruct((n_tok, D_MODEL), F32),
        scratch_shapes=[pltpu.VMEM((tm, D_FF), BF16)],
        compiler_params=pltpu.CompilerParams(vmem_limit_bytes=VMEM_LIMIT),
        name="outffn",
    )(x2d, ya, yb, mod, n2, fg, w_out, w_gate, w_up, w_down)


def _trunk_layer(x, mod, row0, row_stride, s_f0, s_b0, p, final_g):
    bsz, t, _ = x.shape
    x2d = x.reshape(bsz * t, D_MODEL)
    qkv, z, ba, yb = _inproj(x2d, mod, p["g1"], p["w_main"], p["w_ba"], p["a_row"], p["dt_row"], p["ln_g"], p["ln_b"], p["w_s"],
                             p["b_s_t"], seq=t, row0=row0, row_stride=row_stride, tm=IN_TM)
    ya, s_f, s_b = _deltanet(qkv.reshape(bsz, t, 3 * W_A), z.reshape(bsz, t, W_A), ba.reshape(bsz, t, GATE_W),
                             p["conv_w"], p["dn_g"], s_f0, s_b0)
    out = _outffn(x2d, ya.reshape(bsz * t, W_A), yb, mod, p["g2"], final_g, p["w_out"], p["w_gate"], p["w_up"],
                  p["w_down"], seq=t, row0=row0, row_stride=row_stride, tm=FFN_TM)
    return out.reshape(bsz, t, D_MODEL), s_f, s_b


def kernel(x_prompt, x_sample, state_fwd, state_bwd, c, c_ctx, w_ada, b_ada, norm1_g, norm2_g, w_in, conv_w,
           a_log, dt_bias, dn_norm_g, sgu_ln_g, sgu_ln_b, sgu_w, sgu_b, w_out, w_gate, w_up, w_down, final_g):
    depth = w_ada.shape[0]
    assert depth == 1, "the final RMSNorm is fused into the last layer; only depth 1 is supported"
    n_lat = c.shape[0]
    assert 1 + n_lat <= MOD_ROWS
    cc = jnp.zeros((MOD_ROWS, D_MODEL), F32).at[0].set(c_ctx).at[1:1 + n_lat].set(c)
    s_zero = jnp.zeros((x_prompt.shape[0], H_A, DK, DV), F32)
    xp, xs = x_prompt, x_sample
    new_f, new_b = [], []
    gate_pad = jnp.zeros((2 * H_A,), F32)
    for l in range(depth):
        w_in_b = w_in[l].astype(BF16)
        n_main = 4 * W_A + 2 * W_B
        p = dict(
            g1=norm1_g[l].reshape(1, D_MODEL), g2=norm2_g[l].reshape(1, D_MODEL),
            w_main=w_in_b[:, :n_main],
            w_ba=jnp.pad(w_in_b[:, n_main:], ((0, 0), (0, GATE_W - 4 * H_A))),
            ln_g=sgu_ln_g[l].reshape(1, W_B), ln_b=sgu_ln_b[l].reshape(1, W_B),
            w_s=sgu_w[l].astype(BF16), b_s_t=jnp.transpose(sgu_b[l]),
            conv_w=conv_w[l],
            a_row=jnp.pad(jnp.concatenate([gate_pad, a_log[l].reshape(-1)]), (0, GATE_W - 4 * H_A)).reshape(1, GATE_W),
            dt_row=jnp.pad(jnp.concatenate([gate_pad, dt_bias[l].reshape(-1)]), (0, GATE_W - 4 * H_A)).reshape(1, GATE_W),
            dn_g=dn_norm_g[l].reshape(1, DV),
            w_out=w_out[l].astype(BF16), w_gate=w_gate[l].astype(BF16), w_up=w_up[l].astype(BF16),
            w_down=w_down[l].astype(BF16),
        )
        mod = _ada(cc, w_ada[l], b_ada[l]).reshape(MOD_ROWS, N_MOD, D_MODEL)
        fg = final_g.reshape(1, D_MODEL)
        xp, sf, sb = _trunk_layer(xp, mod, 0, 0, s_zero, s_zero, p, fg)
        new_f.append(sf)
        new_b.append(sb)
        xs, _, _ = _trunk_layer(xs, mod, 1, 1, state_fwd[:, l], state_bwd[:, l], p, fg)
    return (xp, xs, jnp.stack(new_f, axis=1), jnp.stack(new_b, axis=1))
```

```python
import functools

import jax
import jax.numpy as jnp
from jax import lax
from jax.experimental import pallas as pl
from jax.experimental.pallas import tpu as pltpu

F32 = jnp.float32
BF16 = jnp.bfloat16

D_MODEL = 1024
W_A = D_MODEL // 2
DK = 128
DV = 128
H_A = W_A // DV
W_B = D_MODEL - W_A
WG = 128
G_B = W_B // WG
SGU_CHUNK = 128
CONV_K = 5
DN_CHUNK = 128
D_FF = 2816
N_MOD = 6
EPS = 1e-6

MOD_ROWS = 16
GATE_W = 128
LANES = 128
INV_BASE = 16
FF_CHUNK = 256
IN_SUB = 256
IN_TM = 512
FFN_SUB = 256
FFN_TM = 512
P1_BATCH = 8
VMEM_LIMIT = 56 * 1024 * 1024


def _sigmoid(x):
    return 0.5 + 0.5 * jnp.tanh(0.5 * x)


def _silu(x):
    hx = 0.5 * x
    return hx + hx * jnp.tanh(hx)


def _gelu_tanh(x):
    c = 0.7978845608028654
    return 0.5 * x * (1.0 + jnp.tanh(c * (x + 0.044715 * (x * x * x))))


def _softplus(x):
    return jnp.maximum(x, 0.0) + jnp.log(1.0 + jnp.exp(-jnp.abs(x)))


def _dot(a, b):
    return jnp.dot(a, b, preferred_element_type=F32)


def _dot_nt(a, b):
    return lax.dot_general(a, b, (((1,), (1,)), ((), ())), preferred_element_type=F32)


def _split3(x):
    hi = x.astype(BF16)
    r1 = x - hi.astype(F32)
    mid = r1.astype(BF16)
    lo = (r1 - mid.astype(F32)).astype(BF16)
    return hi, mid, lo


def _dot_exact_lhs(a_bf16, x):
    hi, mid, lo = _split3(x)
    return _dot(a_bf16, hi) + _dot(a_bf16, mid) + _dot(a_bf16, lo)


def _ada_kernel(c_ref, w_ref, b_ref, o_ref):
    s = _silu(c_ref[...])
    o_ref[...] = _dot(s.astype(BF16), w_ref[...].astype(BF16)) + b_ref[...]


def _ada(cc, w_ada, b_ada):
    n_out = w_ada.shape[1]
    bn = D_MODEL
    return pl.pallas_call(
        _ada_kernel,
        grid=(n_out // bn,),
        in_specs=[
            pl.BlockSpec((MOD_ROWS, D_MODEL), lambda j: (0, 0)),
            pl.BlockSpec((D_MODEL, bn), lambda j: (0, j)),
            pl.BlockSpec((1, bn), lambda j: (0, j)),
        ],
        out_specs=pl.BlockSpec((MOD_ROWS, bn), lambda j: (0, j)),
        out_shape=jax.ShapeDtypeStruct((MOD_ROWS, n_out), F32),
        compiler_params=pltpu.CompilerParams(vmem_limit_bytes=VMEM_LIMIT),
        name="ada",
    )(cc, w_ada, b_ada.reshape(1, n_out))


def _inproj_kernel(x_ref, mod_ref, g1_ref, wm_ref, wba_ref, arow_ref, dtrow_ref, lng_ref, lnb_ref, ws_ref, bs_ref,
                   qkv_ref, z_ref, ba_ref, yb_ref, *, tm):
    subs = [slice(r, r + IN_SUB) for r in range(0, tm, IN_SUB)]
    ri = lax.broadcasted_iota(jnp.int32, (IN_SUB, IN_SUB), 0)
    ci = lax.broadcasted_iota(jnp.int32, (IN_SUB, IN_SUB), 1)
    same_chunk = (ri // DN_CHUNK) == (ci // DN_CHUNK)
    prefix_op = (same_chunk & (ri >= ci)).astype(BF16)
    suffix_op = (same_chunk & (ri <= ci)).astype(BF16)

    def norm(r):
        x = x_ref[r, :]
        ms = jnp.mean(x * x, axis=-1, keepdims=True)
        xn = x * lax.rsqrt(ms + EPS) * g1_ref[...]
        return (xn * (1.0 + mod_ref[1:2, :]) + mod_ref[0:1, :]).astype(BF16)

    def matmul_tasks(r, hb):
        def qkv_piece(j):
            def run():
                l = slice(j * W_A, (j + 1) * W_A)
                qkv_ref[r, l] = _dot(hb, wm_ref[:, l]).astype(BF16)
            return run

        def z_piece():
            z_ref[r, :] = _dot(hb, wm_ref[:, 3 * W_A:4 * W_A]).astype(BF16)

        def gate_piece():
            logit = _dot(hb, wba_ref[...])
            lane = lax.broadcasted_iota(jnp.int32, logit.shape, 1)
            g_all = -jnp.exp(arow_ref[...]) * _softplus(logit + dtrow_ref[...])
            cum_f = _dot_exact_lhs(prefix_op, g_all)
            cum_b = _dot_exact_lhs(suffix_op, g_all)
            ba_ref[r, :] = jnp.where(lane < 2 * H_A, _sigmoid(logit), jnp.where(lane < 3 * H_A, cum_f, cum_b))

        return [qkv_piece(0), qkv_piece(1), qkv_piece(2), z_piece, gate_piece]

    def sgu_tasks(r, st):
        def gelu_u():
            st["u"] = _gelu_tanh(st["u"])

        def gelu_v():
            st["v"] = _gelu_tanh(st["v"])

        def layernorm_v():
            vb = st["v"]
            mu = jnp.mean(vb, axis=-1, keepdims=True)
            vc = vb - mu
            var = jnp.mean(vc * vc, axis=-1, keepdims=True)
            st["v"] = (vc * lax.rsqrt(var + EPS) * lng_ref[...] + lnb_ref[...]).astype(BF16)

        def mix(c):
            def run():
                rr = slice(c * SGU_CHUNK, (c + 1) * SGU_CHUNK)
                ro = slice(r.start + c * SGU_CHUNK, r.start + (c + 1) * SGU_CHUNK)
                for g in range(G_B):
                    l = slice(g * WG, (g + 1) * WG)
                    mixed = _dot(ws_ref[g], st["v"][rr, l]) + bs_ref[:, g:g + 1]
                    yb_ref[ro, l] = (st["u"][rr, l] * mixed).astype(BF16)
            return run

        return [gelu_u, gelu_v, layernorm_v] + [mix(c) for c in range(IN_SUB // SGU_CHUNK)]

    hb = norm(subs[0])
    for i, r in enumerate(subs):
        st = {"u": _dot(hb, wm_ref[:, 4 * W_A:4 * W_A + W_B]),
              "v": _dot(hb, wm_ref[:, 4 * W_A + W_B:4 * W_A + 2 * W_B])}
        mm = matmul_tasks(r, hb)
        ew = sgu_tasks(r, st)
        nxt = {}
        if i + 1 < len(subs):
            ew.append(lambda i=i: nxt.update(hb=norm(subs[i + 1])))
        for k in range(max(len(mm), len(ew))):
            if k < len(mm):
                mm[k]()
            if k < len(ew):
                ew[k]()
        hb = nxt.get("hb")


def _inproj(x2d, mod, g1, w_main, w_ba, a_row, dt_row, ln_g, ln_b, w_s, b_s_t, *, seq, row0, row_stride, tm):
    n_tok = x2d.shape[0]
    assert row_stride == 0 or seq % tm == 0

    def mod_idx(i):
        return (row0 + ((i * tm) // seq) * row_stride, 0, 0)

    const2 = lambda i: (0, 0)
    tok = lambda i: (i, 0)
    return pl.pallas_call(
        functools.partial(_inproj_kernel, tm=tm),
        grid=(n_tok // tm,),
        in_specs=[
            pl.BlockSpec((tm, D_MODEL), tok),
            pl.BlockSpec((None, N_MOD, D_MODEL), mod_idx),
            pl.BlockSpec((1, D_MODEL), const2),
            pl.BlockSpec(w_main.shape, const2, pipeline_mode=pl.Buffered(1)),
            pl.BlockSpec(w_ba.shape, const2),
            pl.BlockSpec((1, GATE_W), const2),
            pl.BlockSpec((1, GATE_W), const2),
            pl.BlockSpec((1, W_B), const2),
            pl.BlockSpec((1, W_B), const2),
            pl.BlockSpec(w_s.shape, lambda i: (0, 0, 0)),
            pl.BlockSpec(b_s_t.shape, const2),
        ],
        out_specs=[
            pl.BlockSpec((tm, 3 * W_A), tok),
            pl.BlockSpec((tm, W_A), tok),
            pl.BlockSpec((tm, GATE_W), tok),
            pl.BlockSpec((tm, W_B), tok),
        ],
        out_shape=[
            jax.ShapeDtypeStruct((n_tok, 3 * W_A), BF16),
            jax.ShapeDtypeStruct((n_tok, W_A), BF16),
            jax.ShapeDtypeStruct((n_tok, GATE_W), F32),
            jax.ShapeDtypeStruct((n_tok, W_B), BF16),
        ],
        compiler_params=pltpu.CompilerParams(vmem_limit_bytes=VMEM_LIMIT),
        name="inproj",
    )(x2d, mod, g1, w_main, w_ba, a_row, dt_row, ln_g, ln_b, w_s, b_s_t)


def _tri_inv_lower(mats):
    n = DN_CHUNK
    nblk = n // INV_BASE
    per = LANES // n
    assert per >= 1 and len(mats) % per == 0
    lane1 = lax.broadcasted_iota(jnp.int32, (INV_BASE, n), 1)
    lane2 = lax.broadcasted_iota(jnp.int32, (INV_BASE, per * n), 1)
    sub2 = lax.broadcasted_iota(jnp.int32, (INV_BASE, per * n), 0)
    blk_masks = [(lane1 // INV_BASE) == rb for rb in range(nblk)]

    def compress(a):
        out = jnp.where(blk_masks[0], a[0:INV_BASE, :], 0.0)
        for rb in range(1, nblk):
            out = out + jnp.where(blk_masks[rb], a[rb * INV_BASE:(rb + 1) * INV_BASE, :], 0.0)
        return out

    packed = [compress(mats[p]) if per == 1 else
              jnp.concatenate([compress(mats[per * p + q]) for q in range(per)], axis=1)
              for p in range(len(mats) // per)]
    eye = ((lane2 % INV_BASE) == sub2).astype(F32)
    xs = [eye for _ in packed]
    grp = (lane2 // INV_BASE) * INV_BASE
    for j in range(INV_BASE - 1):
        cols = [jnp.take_along_axis(nc, grp + j, axis=1) for nc in packed]
        xs = [x - c * x[j:j + 1, :] for x, c in zip(xs, cols)]

    def expand(xc):
        return jnp.concatenate([jnp.where(m, xc, 0.0) for m in blk_masks], axis=0)

    ds = []
    for x in xs:
        for q in range(per):
            ds.append(expand(x[:, q * n:(q + 1) * n]))
    row = lax.broadcasted_iota(jnp.int32, (n, n), 0)
    col = lax.broadcasted_iota(jnp.int32, (n, n), 1)
    bs = INV_BASE
    while bs < n:
        cmask = ((row // (2 * bs)) == (col // (2 * bs))) & (((row // bs) % 2) == 1) & (((col // bs) % 2) == 0)
        cs = [jnp.where(cmask, a, 0.0).astype(BF16) for a in mats]
        dbs = [d.astype(BF16) for d in ds]
        t1 = [_dot(db, c) for db, c in zip(dbs, cs)]
        t2 = [_dot(t.astype(BF16), db) for t, db in zip(t1, dbs)]
        ds = [d - t for d, t in zip(ds, t2)]
        bs *= 2
    return ds


def _dn_kernel(q_ref, k_ref, v_ref, z_ref, ba_ref, cwq_ref, cwk_ref, cwv_ref,
               ng_ref, sf0_ref, sb0_ref,
               ya_ref, sf_ref, sb_ref,
               pad_ref, qs_ref, ks_ref, vs_ref,
               r_ref, mq_ref, et_ref, o_ref, *, seq):
    t = seq
    c_len = DN_CHUNK
    n = t // c_len
    h = pl.program_id(1)
    rb = min(t, 256)

    zeros8 = jnp.zeros((8, DK), F32)
    pad_ref[0:8, :] = zeros8
    pad_ref[8 + t:16 + t, :] = zeros8

    def conv_silu(x_ref, cw_ref, dst_ref, l2, scale):
        pad_ref[8:8 + t, :] = x_ref[...].astype(F32)
        for r in range(0, t, rb):
            acc = cw_ref[0:1, :] * pad_ref[6 + r:6 + r + rb, :]
            for j in range(1, CONV_K):
                acc = acc + cw_ref[j:j + 1, :] * pad_ref[6 + j + r:6 + j + r + rb, :]
            y = _silu(acc)
            if l2:
                y = y * (lax.rsqrt(jnp.sum(y * y, axis=-1, keepdims=True) + EPS) * scale)
            dst_ref[r:r + rb, :] = y

    conv_silu(q_ref, cwq_ref, qs_ref, True, DK ** -0.5)
    conv_silu(k_ref, cwk_ref, ks_ref, True, 1.0)
    conv_silu(v_ref, cwv_ref, vs_ref, False, 1.0)

    row = lax.broadcasted_iota(jnp.int32, (c_len, c_len), 0)
    col = lax.broadcasted_iota(jnp.int32, (c_len, c_len), 1)
    lower_incl = row >= col
    upper_incl = row <= col
    lower_strict = row > col
    gate_lane = jnp.full((c_len, GATE_W), h, jnp.int32)
    nb = min(P1_BATCH, n)
    assert n % nb == 0

    def phase1(it, carry):
        cs = [it * nb + i for i in range(nb)]
        r64 = [pl.multiple_of(c * c_len, c_len) for c in cs]
        r128 = [pl.multiple_of(c * DK, DK) for c in cs]
        r192 = [pl.multiple_of(c * (DK + c_len), DK + c_len) for c in cs]
        r8 = [pl.multiple_of(c * 8, 8) for c in cs]
        items = [(i, d) for i in range(nb) for d in range(2)]
        q = [qs_ref[pl.ds(r, c_len), :] for r in r64]
        k = [ks_ref[pl.ds(r, c_len), :] for r in r64]
        v = [vs_ref[pl.ds(r, c_len), :] for r in r64]
        kb = [x.astype(BF16) for x in k]
        kk = [_dot_nt(x, x) for x in kb]
        qk = [_dot_nt(x.astype(BF16), y) for x, y in zip(q, kb)]
        gates = [ba_ref[pl.ds(r, c_len), :] for r in r64]
        beta = [jnp.take_along_axis(gates[i], gate_lane + d * H_A, axis=1) for i, d in items]
        cum = [jnp.take_along_axis(gates[i], gate_lane + (2 + d) * H_A, axis=1) for i, d in items]
        cum_t = [x[:, 0:c_len].T for x in cum]
        dm = [x[:, 0:c_len] - y for x, y in zip(cum, cum_t)]
        dec_sys = [jnp.where(lower_incl, jnp.exp(x if d == 0 else -x), 0.0) for (i, d), x in zip(items, dm)]
        dec_qk = [ds if d == 0 else jnp.where(upper_incl, jnp.exp(x), 0.0)
                  for (i, d), x, ds in zip(items, dm, dec_sys)]
        a = [jnp.where(lower_strict, b[:, 0:c_len] * kk[i] * ds, 0.0) for (i, d), b, ds in zip(items, beta, dec_sys)]
        tm = [x.astype(BF16) for x in _tri_inv_lower(a)]
        e_col = [jnp.exp(x) for x in cum]
        rhs = []
        for (i, d), b, e in zip(items, beta, e_col):
            if d == 0:
                rhs.append(jnp.concatenate([v[i] * b, k[i] * (b * e)], axis=1).astype(BF16))
            else:
                rhs.append(jnp.concatenate([v[i], k[i] * e], axis=1).astype(BF16))
        uw = []
        for (i, d), t_, r_, b in zip(items, tm, rhs, beta):
            if d == 0:
                uw.append(_dot(t_, r_))
            else:
                uw.append(jnp.concatenate([b, b], axis=1)
                          * lax.dot_general(t_, r_, (((0,), (0,)), ((), ())), preferred_element_type=F32))
        tot = [x[c_len - 1:c_len, :] if d == 0 else x[0:1, :] for (i, d), x in zip(items, cum)]
        k_dec_t = [(k[i] * jnp.exp(t_ - x)).T.astype(BF16) for (i, d), x, t_ in zip(items, cum, tot)]
        qkd = [(qk[i] * dq).astype(BF16) for (i, d), dq in zip(items, dec_qk)]
        uwb = [x.astype(BF16) for x in uw]
        kuw = [_dot(kt, x) for kt, x in zip(k_dec_t, uwb)]
        quw = [_dot(qd, x) for qd, x in zip(qkd, uwb)]
        for idx, (i, d) in enumerate(items):
            r_ref[d, pl.ds(r128[i], DK), :] = kuw[idx][:, 0:DV]
            mq_ref[d, pl.ds(r192[i], DK), :] = (-kuw[idx][:, DV:DV + DK]).astype(BF16)
            mq_ref[d, pl.ds(r192[i] + DK, c_len), :] = (q[i] * e_col[idx] - quw[idx][:, DV:DV + DK]).astype(BF16)
            o_ref[d, pl.ds(r64[i], c_len), :] = quw[idx][:, 0:DV]
            et_ref[d, pl.ds(r8[i], 8), :] = jnp.exp(jnp.broadcast_to(tot[idx], (8, DK)))
        return carry

    lax.fori_loop(0, n // nb, phase1, 0)

    def phase2(i, carry):
        new = []
        for d in range(2):
            s = carry[d]
            c = i if d == 0 else n - 1 - i
            r64 = pl.multiple_of(c * c_len, c_len)
            r128 = pl.multiple_of(c * DK, DK)
            r192 = pl.multiple_of(c * (DK + c_len), DK + c_len)
            ms = _dot(mq_ref[d, pl.ds(r192, DK + c_len), :], s.astype(BF16))
            et = et_ref[d, pl.ds(pl.multiple_of(c * 8, 8), 8), :]
            new.append(s * et[0:1, :] + r_ref[d, pl.ds(r128, DK), :] + ms[0:DK])
            o_ref[d, pl.ds(r64, c_len), :] = o_ref[d, pl.ds(r64, c_len), :] + ms[DK:DK + c_len]
        return tuple(new)

    s_f, s_b = lax.fori_loop(0, n, phase2, (sf0_ref[...], sb0_ref[...]))
    sf_ref[...] = s_f
    sb_ref[...] = s_b

    for r in range(0, t, rb):
        o = o_ref[0, r:r + rb, :] + o_ref[1, r:r + rb, :]
        ms = jnp.mean(o * o, axis=-1, keepdims=True)
        y = o * lax.rsqrt(ms + EPS) * ng_ref[...]
        ya_ref[r:r + rb, :] = (y * _silu(z_ref[r:r + rb, :].astype(F32))).astype(BF16)


def _deltanet(qkv, z, ba, conv_w, norm_g, s_f0, s_b0):
    bsz, t, _ = qkv.shape
    n = t // DN_CHUNK
    col = lambda off: (lambda b, h: (b, 0, off + h))
    const2 = lambda b, h: (0, 0)
    cw = lambda off: (lambda b, h: (0, off + h))
    state = lambda b, h: (b, h, 0, 0)
    return pl.pallas_call(
        functools.partial(_dn_kernel, seq=t),
        grid=(bsz, H_A),
        in_specs=[
            pl.BlockSpec((None, t, DK), col(0)),
            pl.BlockSpec((None, t, DK), col(H_A)),
            pl.BlockSpec((None, t, DV), col(2 * H_A)),
            pl.BlockSpec((None, t, DV), col(0)),
            pl.BlockSpec((None, t, GATE_W), lambda b, h: (b, 0, 0)),
            pl.BlockSpec((CONV_K, DK), cw(0)),
            pl.BlockSpec((CONV_K, DK), cw(H_A)),
            pl.BlockSpec((CONV_K, DV), cw(2 * H_A)),
            pl.BlockSpec((1, DV), const2),
            pl.BlockSpec((None, None, DK, DV), state),
            pl.BlockSpec((None, None, DK, DV), state),
        ],
        out_specs=[
            pl.BlockSpec((None, t, DV), col(0)),
            pl.BlockSpec((None, None, DK, DV), state),
            pl.BlockSpec((None, None, DK, DV), state),
        ],
        out_shape=[
            jax.ShapeDtypeStruct((bsz, t, W_A), BF16),
            jax.ShapeDtypeStruct((bsz, H_A, DK, DV), F32),
            jax.ShapeDtypeStruct((bsz, H_A, DK, DV), F32),
        ],
        scratch_shapes=[
            pltpu.VMEM((t + 16, DK), F32),
            pltpu.VMEM((t, DK), F32),
            pltpu.VMEM((t, DK), F32),
            pltpu.VMEM((t, DV), F32),
            pltpu.VMEM((2, n * DK, DV), F32),
            pltpu.VMEM((2, n * (DK + DN_CHUNK), DK), BF16),
            pltpu.VMEM((2, n * 8, DV), F32),
            pltpu.VMEM((2, t, DV), F32),
        ],
        compiler_params=pltpu.CompilerParams(vmem_limit_bytes=VMEM_LIMIT),
        name="deltanet",
    )(qkv, qkv, qkv, z, ba, conv_w, conv_w, conv_w, norm_g, s_f0, s_b0)


def _outffn_kernel(x_ref, ya_ref, yb_ref, mod_ref, n2_ref, fg_ref, wo_ref, wg_ref, wu_ref, wd_ref, o_ref,
                   act_ref, *, tm):
    subs = [slice(r, r + FFN_SUB) for r in range(0, tm, FFN_SUB)]
    hb = []
    for r in subs:
        y = _dot(ya_ref[r, :], wo_ref[0:W_A, :]) + _dot(yb_ref[r, :], wo_ref[W_A:D_MODEL, :])
        x1 = x_ref[r, :] + mod_ref[2:3, :] * y
        o_ref[r, :] = x1
        ms = jnp.mean(x1 * x1, axis=-1, keepdims=True)
        h = (x1 * lax.rsqrt(ms + EPS) * n2_ref[...]) * (1.0 + mod_ref[4:5, :]) + mod_ref[3:4, :]
        hb.append(h.astype(BF16))
    for c in range(D_FF // FF_CHUNK):
        l = slice(c * FF_CHUNK, (c + 1) * FF_CHUNK)
        for r, h in zip(subs, hb):
            act_ref[r, l] = (_silu(_dot(h, wg_ref[:, l])) * _dot(h, wu_ref[:, l])).astype(BF16)
    for r in subs:
        x2 = o_ref[r, :] + mod_ref[5:6, :] * _dot(act_ref[r, :], wd_ref[...])
        ms2 = jnp.mean(x2 * x2, axis=-1, keepdims=True)
        o_ref[r, :] = x2 * lax.rsqrt(ms2 + EPS) * fg_ref[...]


def _outffn(x2d, ya, yb, mod, n2, fg, w_out, w_gate, w_up, w_down, *, seq, row0, row_stride, tm):
    n_tok = x2d.shape[0]
    assert row_stride == 0 or seq % tm == 0

    def mod_idx(i):
        return (row0 + ((i * tm) // seq) * row_stride, 0, 0)

    const2 = lambda i: (0, 0)
    tok = lambda i: (i, 0)
    resident = lambda w: pl.BlockSpec(w.shape, const2, pipeline_mode=pl.Buffered(1))
    return pl.pallas_call(
        functools.partial(_outffn_kernel, tm=tm),
        grid=(n_tok // tm,),
        in_specs=[
            pl.BlockSpec((tm, D_MODEL), tok),
            pl.BlockSpec((tm, W_A), tok),
            pl.BlockSpec((tm, W_B), tok),
            pl.BlockSpec((None, N_MOD, D_MODEL), mod_idx),
            pl.BlockSpec((1, D_MODEL), const2),
            pl.BlockSpec((1, D_MODEL), const2),
            resident(w_out), resident(w_gate), resident(w_up), resident(w_down),
        ],
        out_specs=pl.BlockSpec((tm, D_MODEL), tok),
        out_shape=jax.ShapeDtypeStruct((n_tok, D_MODEL), F32),
        scratch_shapes=[pltpu.VMEM((tm, D_FF), BF16)],
        compiler_params=pltpu.CompilerParams(vmem_limit_bytes=VMEM_LIMIT),
        name="outffn",
    )(x2d, ya, yb, mod, n2, fg, w_out, w_gate, w_up, w_down)


def _trunk_layer(x, mod, row0, row_stride, s_f0, s_b0, p, final_g):
    bsz, t, _ = x.shape
    x2d = x.reshape(bsz * t, D_MODEL)
    qkv, z, ba, yb = _inproj(x2d, mod, p["g1"], p["w_main"], p["w_ba"], p["a_row"], p["dt_row"], p["ln_g"], p["ln_b"], p["w_s"],
                             p["b_s_t"], seq=t, row0=row0, row_stride=row_stride, tm=IN_TM)
    ya, s_f, s_b = _deltanet(qkv.reshape(bsz, t, 3 * W_A), z.reshape(bsz, t, W_A), ba.reshape(bsz, t, GATE_W),
                             p["conv_w"], p["dn_g"], s_f0, s_b0)
    out = _outffn(x2d, ya.reshape(bsz * t, W_A), yb, mod, p["g2"], final_g, p["w_out"], p["w_gate"], p["w_up"],
                  p["w_down"], seq=t, row0=row0, row_stride=row_stride, tm=FFN_TM)
    return out.reshape(bsz, t, D_MODEL), s_f, s_b


def kernel(x_prompt, x_sample, state_fwd, state_bwd, c, c_ctx, w_ada, b_ada, norm1_g, norm2_g, w_in, conv_w,
           a_log, dt_bias, dn_norm_g, sgu_ln_g, sgu_ln_b, sgu_w, sgu_b, w_out, w_gate, w_up, w_down, final_g):
    depth = w_ada.shape[0]
    assert depth == 1, "the final RMSNorm is fused into the last layer; only depth 1 is supported"
    n_lat = c.shape[0]
    assert 1 + n_lat <= MOD_ROWS
    cc = jnp.zeros((MOD_ROWS, D_MODEL), F32).at[0].set(c_ctx).at[1:1 + n_lat].set(c)
    s_zero = jnp.zeros((x_prompt.shape[0], H_A, DK, DV), F32)
    xp, xs = x_prompt, x_sample
    new_f, new_b = [], []
    gate_pad = jnp.zeros((2 * H_A,), F32)
    for l in range(depth):
        w_in_b = w_in[l].astype(BF16)
        n_main = 4 * W_A + 2 * W_B
        p = dict(
            g1=norm1_g[l].reshape(1, D_MODEL), g2=norm2_g[l].reshape(1, D_MODEL),
            w_main=w_in_b[:, :n_main],
            w_ba=jnp.pad(w_in_b[:, n_main:], ((0, 0), (0, GATE_W - 4 * H_A))),
            ln_g=sgu_ln_g[l].reshape(1, W_B), ln_b=sgu_ln_b[l].reshape(1, W_B),
            w_s=sgu_w[l].astype(BF16), b_s_t=jnp.transpose(sgu_b[l]),
            conv_w=conv_w[l],
            a_row=jnp.pad(jnp.concatenate([gate_pad, a_log[l].reshape(-1)]), (0, GATE_W - 4 * H_A)).reshape(1, GATE_W),
            dt_row=jnp.pad(jnp.concatenate([gate_pad, dt_bias[l].reshape(-1)]), (0, GATE_W - 4 * H_A)).reshape(1, GATE_W),
            dn_g=dn_norm_g[l].reshape(1, DV),
            w_out=w_out[l].astype(BF16), w_gate=w_gate[l].astype(BF16), w_up=w_up[l].astype(BF16),
            w_down=w_down[l].astype(BF16),
        )
        mod = _ada(cc, w_ada[l], b_ada[l]).reshape(MOD_ROWS, N_MOD, D_MODEL)
        fg = final_g.reshape(1, D_MODEL)
        xp, sf, sb = _trunk_layer(xp, mod, 0, 0, s_zero, s_zero, p, fg)
        new_f.append(sf)
        new_b.append(sb)
        xs, _, _ = _trunk_layer(xs, mod, 1, 1, state_fwd[:, l], state_bwd[:, l], p, fg)
    return (xp, xs, jnp.stack(new_f, axis=1), jnp.stack(new_b, axis=1))
```

```python
import functools

import jax
import jax.numpy as jnp
from jax import lax
from jax.experimental import pallas as pl
from jax.experimental.pallas import tpu as pltpu

F32 = jnp.float32
BF16 = jnp.bfloat16

D_MODEL = 1024
W_A = D_MODEL // 2
DK = 128
DV = 128
H_A = W_A // DV
W_B = D_MODEL - W_A
WG = 128
G_B = W_B // WG
SGU_CHUNK = 128
CONV_K = 5
DN_CHUNK = 128
D_FF = 2816
N_MOD = 6
EPS = 1e-6

MOD_ROWS = 16
GATE_W = 128
LANES = 128
INV_BASE = 16
FF_CHUNK = 256
IN_SUB = 256
IN_TM = 512
FFN_SUB = 256
FFN_TM = 512
P1_BATCH = 8
VMEM_LIMIT = 56 * 1024 * 1024


def _sigmoid(x):
    return 0.5 + 0.5 * jnp.tanh(0.5 * x)


def _silu(x):
    hx = 0.5 * x
    return hx + hx * jnp.tanh(hx)


def _gelu_tanh(x):
    c = 0.7978845608028654
    return 0.5 * x * (1.0 + jnp.tanh(c * (x + 0.044715 * (x * x * x))))


def _softplus(x):
    return jnp.maximum(x, 0.0) + jnp.log(1.0 + jnp.exp(-jnp.abs(x)))


def _dot(a, b):
    return jnp.dot(a, b, preferred_element_type=F32)


def _dot_nt(a, b):
    return lax.dot_general(a, b, (((1,), (1,)), ((), ())), preferred_element_type=F32)


def _split3(x):
    hi = x.astype(BF16)
    r1 = x - hi.astype(F32)
    mid = r1.astype(BF16)
    lo = (r1 - mid.astype(F32)).astype(BF16)
    return hi, mid, lo


def _dot_exact_lhs(a_bf16, x):
    hi, mid, lo = _split3(x)
    return _dot(a_bf16, hi) + _dot(a_bf16, mid) + _dot(a_bf16, lo)


def _ada_kernel(c_ref, w_ref, b_ref, o_ref):
    s = _silu(c_ref[...])
    o_ref[...] = _dot(s.astype(BF16), w_ref[...].astype(BF16)) + b_ref[...]


def _ada(cc, w_ada, b_ada):
    n_out = w_ada.shape[1]
    bn = D_MODEL
    return pl.pallas_call(
        _ada_kernel,
        grid=(n_out // bn,),
        in_specs=[
            pl.BlockSpec((MOD_ROWS, D_MODEL), lambda j: (0, 0)),
            pl.BlockSpec((D_MODEL, bn), lambda j: (0, j)),
            pl.BlockSpec((1, bn), lambda j: (0, j)),
        ],
        out_specs=pl.BlockSpec((MOD_ROWS, bn), lambda j: (0, j)),
        out_shape=jax.ShapeDtypeStruct((MOD_ROWS, n_out), F32),
        compiler_params=pltpu.CompilerParams(vmem_limit_bytes=VMEM_LIMIT),
        name="ada",
    )(cc, w_ada, b_ada.reshape(1, n_out))


def _inproj_kernel(x_ref, mod_ref, g1_ref, wm_ref, wba_ref, arow_ref, dtrow_ref, lng_ref, lnb_ref, ws_ref, bs_ref,
                   qkv_ref, z_ref, ba_ref, yb_ref, *, tm):
    subs = [slice(r, r + IN_SUB) for r in range(0, tm, IN_SUB)]
    ri = lax.broadcasted_iota(jnp.int32, (IN_SUB, IN_SUB), 0)
    ci = lax.broadcasted_iota(jnp.int32, (IN_SUB, IN_SUB), 1)
    same_chunk = (ri // DN_CHUNK) == (ci // DN_CHUNK)
    prefix_op = (same_chunk & (ri >= ci)).astype(BF16)
    suffix_op = (same_chunk & (ri <= ci)).astype(BF16)

    def norm(r):
        x = x_ref[r, :]
        ms = jnp.mean(x * x, axis=-1, keepdims=True)
        xn = x * lax.rsqrt(ms + EPS) * g1_ref[...]
        return (xn * (1.0 + mod_ref[1:2, :]) + mod_ref[0:1, :]).astype(BF16)

    def matmul_tasks(r, hb):
        def qkv_piece(j):
            def run():
                l = slice(j * W_A, (j + 1) * W_A)
                qkv_ref[r, l] = _dot(hb, wm_ref[:, l]).astype(BF16)
            return run

        def z_piece():
            z_ref[r, :] = _dot(hb, wm_ref[:, 3 * W_A:4 * W_A]).astype(BF16)

        def gate_piece():
            logit = _dot(hb, wba_ref[...])
            lane = lax.broadcasted_iota(jnp.int32, logit.shape, 1)
            g_all = -jnp.exp(arow_ref[...]) * _softplus(logit + dtrow_ref[...])
            cum_f = _dot_exact_lhs(prefix_op, g_all)
            cum_b = _dot_exact_lhs(suffix_op, g_all)
            ba_ref[r, :] = jnp.where(lane < 2 * H_A, _sigmoid(logit), jnp.where(lane < 3 * H_A, cum_f, cum_b))

        return [qkv_piece(0), qkv_piece(1), qkv_piece(2), z_piece, gate_piece]

    def sgu_tasks(r, st):
        def gelu_u():
            st["u"] = _gelu_tanh(st["u"])

        def gelu_v():
            st["v"] = _gelu_tanh(st["v"])

        def layernorm_v():
            vb = st["v"]
            mu = jnp.mean(vb, axis=-1, keepdims=True)
            vc = vb - mu
            var = jnp.mean(vc * vc, axis=-1, keepdims=True)
            st["v"] = (vc * lax.rsqrt(var + EPS) * lng_ref[...] + lnb_ref[...]).astype(BF16)

        def mix(c):
            def run():
                rr = slice(c * SGU_CHUNK, (c + 1) * SGU_CHUNK)
                ro = slice(r.start + c * SGU_CHUNK, r.start + (c + 1) * SGU_CHUNK)
                for g in range(G_B):
                    l = slice(g * WG, (g + 1) * WG)
                    mixed = _dot(ws_ref[g], st["v"][rr, l]) + bs_ref[:, g:g + 1]
                    yb_ref[ro, l] = (st["u"][rr, l] * mixed).astype(BF16)
            return run

        return [gelu_u, gelu_v, layernorm_v] + [mix(c) for c in range(IN_SUB // SGU_CHUNK)]

    hb = norm(subs[0])
    for i, r in enumerate(subs):
        st = {"u": _dot(hb, wm_ref[:, 4 * W_A:4 * W_A + W_B]),
              "v": _dot(hb, wm_ref[:, 4 * W_A + W_B:4 * W_A + 2 * W_B])}
        mm = matmul_tasks(r, hb)
        ew = sgu_tasks(r, st)
        nxt = {}
        if i + 1 < len(subs):
            ew.append(lambda i=i: nxt.update(hb=norm(subs[i + 1])))
        for k in range(max(len(mm), len(ew))):
            if k < len(mm):
                mm[k]()
            if k < len(ew):
                ew[k]()
        hb = nxt.get("hb")


def _inproj(x2d, mod, g1, w_main, w_ba, a_row, dt_row, ln_g, ln_b, w_s, b_s_t, *, seq, row0, row_stride, tm):
    n_tok = x2d.shape[0]
    assert row_stride == 0 or seq % tm == 0

    def mod_idx(i):
        return (row0 + ((i * tm) // seq) * row_stride, 0, 0)

    const2 = lambda i: (0, 0)
    tok = lambda i: (i, 0)
    return pl.pallas_call(
        functools.partial(_inproj_kernel, tm=tm),
        grid=(n_tok // tm,),
        in_specs=[
            pl.BlockSpec((tm, D_MODEL), tok),
            pl.BlockSpec((None, N_MOD, D_MODEL), mod_idx),
            pl.BlockSpec((1, D_MODEL), const2),
            pl.BlockSpec(w_main.shape, const2, pipeline_mode=pl.Buffered(1)),
            pl.BlockSpec(w_ba.shape, const2),
            pl.BlockSpec((1, GATE_W), const2),
            pl.BlockSpec((1, GATE_W), const2),
            pl.BlockSpec((1, W_B), const2),
            pl.BlockSpec((1, W_B), const2),
            pl.BlockSpec(w_s.shape, lambda i: (0, 0, 0)),
            pl.BlockSpec(b_s_t.shape, const2),
        ],
        out_specs=[
            pl.BlockSpec((tm, 3 * W_A), tok),
            pl.BlockSpec((tm, W_A), tok),
            pl.BlockSpec((tm, GATE_W), tok),
            pl.BlockSpec((tm, W_B), tok),
        ],
        out_shape=[
            jax.ShapeDtypeStruct((n_tok, 3 * W_A), BF16),
            jax.ShapeDtypeStruct((n_tok, W_A), BF16),
            jax.ShapeDtypeStruct((n_tok, GATE_W), F32),
            jax.ShapeDtypeStruct((n_tok, W_B), BF16),
        ],
        compiler_params=pltpu.CompilerParams(vmem_limit_bytes=VMEM_LIMIT),
        name="inproj",
    )(x2d, mod, g1, w_main, w_ba, a_row, dt_row, ln_g, ln_b, w_s, b_s_t)


def _compress_diag_blocks(a):
    n = a.shape[0]
    lane = lax.broadcasted_iota(jnp.int32, (INV_BASE, n), 1)
    out = jnp.where((lane // INV_BASE) == 0, a[0:INV_BASE, :], 0.0)
    for rb in range(1, n // INV_BASE):
        out = out + jnp.where((lane // INV_BASE) == rb, a[rb * INV_BASE:(rb + 1) * INV_BASE, :], 0.0)
    return out


def _tri_inv_lower(diag_blocks, mats_bf16):
    n = DN_CHUNK
    nblk = n // INV_BASE
    per = LANES // n
    assert per >= 1 and len(diag_blocks) % per == 0
    lane1 = lax.broadcasted_iota(jnp.int32, (INV_BASE, n), 1)
    lane2 = lax.broadcasted_iota(jnp.int32, (INV_BASE, per * n), 1)
    sub2 = lax.broadcasted_iota(jnp.int32, (INV_BASE, per * n), 0)
    packed = [diag_blocks[p] if per == 1 else
              jnp.concatenate([diag_blocks[per * p + q] for q in range(per)], axis=1)
              for p in range(len(diag_blocks) // per)]
    eye = ((lane2 % INV_BASE) == sub2).astype(F32)
    xs = [eye for _ in packed]
    grp = (lane2 // INV_BASE) * INV_BASE
    for j in range(INV_BASE - 1):
        cols = [jnp.take_along_axis(nc, grp + j, axis=1) for nc in packed]
        xs = [x - c * x[j:j + 1, :] for x, c in zip(xs, cols)]

    def expand(xc):
        return jnp.concatenate([jnp.where((lane1 // INV_BASE) == rb, xc, 0.0) for rb in range(nblk)], axis=0)

    ds = []
    for x in xs:
        for q in range(per):
            ds.append(expand(x[:, q * n:(q + 1) * n]))
    row = lax.broadcasted_iota(jnp.int32, (n, n), 0)
    col = lax.broadcasted_iota(jnp.int32, (n, n), 1)
    zero = jnp.zeros((n, n), BF16)
    bs = INV_BASE
    while bs < n:
        cmask = ((row // (2 * bs)) == (col // (2 * bs))) & (((row // bs) % 2) == 1) & (((col // bs) % 2) == 0)
        cs = [jnp.where(cmask, a, zero) for a in mats_bf16]
        dbs = [d.astype(BF16) for d in ds]
        t1 = [_dot(db, c) for db, c in zip(dbs, cs)]
        t2 = [_dot(t.astype(BF16), db) for t, db in zip(t1, dbs)]
        ds = [d - t for d, t in zip(ds, t2)]
        bs *= 2
    return ds


def _dn_kernel(q_ref, k_ref, v_ref, z_ref, ba_ref, cwq_ref, cwk_ref, cwv_ref,
               ng_ref, sf0_ref, sb0_ref,
               ya_ref, sf_ref, sb_ref,
               padq_ref, padk_ref, padv_ref, qs_ref, ks_ref, vs_ref,
               r_ref, mq_ref, et_ref, o_ref, *, seq):
    t = seq
    c_len = DN_CHUNK
    n = t // c_len
    step = pl.program_id(0)
    h = jnp.maximum(step - 1, 0) % H_A
    slot_new = step % 2
    slot_cur = 1 - slot_new
    rb = min(t, 256)

    zeros8 = jnp.zeros((8, DK), F32)
    for src_ref, pad_ref in ((q_ref, padq_ref), (k_ref, padk_ref), (v_ref, padv_ref)):
        pad_ref[0:8, :] = zeros8
        pad_ref[8 + t:16 + t, :] = zeros8
        for r in range(0, t, rb):
            pad_ref[8 + r:8 + r + rb, :] = src_ref[r:r + rb, :].astype(F32)

    def conv_block(i):
        r0 = pl.multiple_of(i * c_len, c_len)
        for pad_ref, cw_ref, dst_ref, l2, scale in ((padq_ref, cwq_ref, qs_ref, True, DK ** -0.5),
                                                    (padk_ref, cwk_ref, ks_ref, True, 1.0),
                                                    (padv_ref, cwv_ref, vs_ref, False, 1.0)):
            acc = cw_ref[0:1, :] * pad_ref[pl.ds(r0 + 6, c_len), :]
            for j in range(1, CONV_K):
                acc = acc + cw_ref[j:j + 1, :] * pad_ref[pl.ds(r0 + 6 + j, c_len), :]
            y = _silu(acc)
            if l2:
                y = y * (lax.rsqrt(jnp.sum(y * y, axis=-1, keepdims=True) + EPS) * scale)
            dst_ref[slot_new, pl.ds(r0, c_len), :] = y

    @pl.when(step == 0)
    def _():
        def body(i, carry):
            conv_block(i)
            return carry
        lax.fori_loop(0, n, body, 0)

    @pl.when(step > 0)
    def _():
        _dn_sequence(z_ref, ba_ref, ng_ref, sf0_ref, sb0_ref, ya_ref, sf_ref, sb_ref,
                     qs_ref.at[slot_cur], ks_ref.at[slot_cur], vs_ref.at[slot_cur],
                     r_ref, mq_ref, et_ref, o_ref, conv_block, h, t)


def _dn_sequence(z_ref, ba_ref, ng_ref, sf0_ref, sb0_ref, ya_ref, sf_ref, sb_ref, qs_ref, ks_ref, vs_ref,
                 r_ref, mq_ref, et_ref, o_ref, conv_block, h, t):
    c_len = DN_CHUNK
    n = t // c_len
    rb = min(t, 256)

    row = lax.broadcasted_iota(jnp.int32, (c_len, c_len), 0)
    col = lax.broadcasted_iota(jnp.int32, (c_len, c_len), 1)
    lower_incl = row >= col
    upper_incl = row <= col
    lower_strict = row > col
    gate_lane = jnp.full((c_len, GATE_W), h, jnp.int32)
    nb = min(P1_BATCH, n)
    assert n % nb == 0

    def phase1(it, carry):
        cs = [it * nb + i for i in range(nb)]
        r64 = [pl.multiple_of(c * c_len, c_len) for c in cs]
        r128 = [pl.multiple_of(c * DK, DK) for c in cs]
        r192 = [pl.multiple_of(c * (DK + c_len), DK + c_len) for c in cs]
        r8 = [pl.multiple_of(c * 8, 8) for c in cs]
        items = [(i, d) for i in range(nb) for d in range(2)]
        q = [qs_ref[pl.ds(r, c_len), :] for r in r64]
        k = [ks_ref[pl.ds(r, c_len), :] for r in r64]
        kb = [x.astype(BF16) for x in k]
        kk = [_dot_nt(x, x) for x in kb]
        qk = [_dot_nt(x.astype(BF16), y) for x, y in zip(q, kb)]
        prep = []
        for i, d in items:
            gates = ba_ref[pl.ds(r64[i], c_len), :]
            b = jnp.take_along_axis(gates, gate_lane + d * H_A, axis=1)
            cum = jnp.take_along_axis(gates, gate_lane + (2 + d) * H_A, axis=1)
            dm = cum[:, 0:c_len] - cum[:, 0:c_len].T
            dec_sys = jnp.where(lower_incl, jnp.exp(dm if d == 0 else -dm), 0.0)
            dec_qk = dec_sys if d == 0 else jnp.where(upper_incl, jnp.exp(dm), 0.0)
            a = jnp.where(lower_strict, b[:, 0:c_len] * kk[i] * dec_sys, 0.0)
            e_col = jnp.exp(cum)
            v = vs_ref[pl.ds(r64[i], c_len), :]
            if d == 0:
                rhs = jnp.concatenate([v * b, k[i] * (b * e_col)], axis=1).astype(BF16)
            else:
                rhs = jnp.concatenate([v, k[i] * e_col], axis=1).astype(BF16)
            tot = cum[c_len - 1:c_len, :] if d == 0 else cum[0:1, :]
            et_ref[d, pl.ds(r8[i], 8), :] = jnp.exp(jnp.broadcast_to(tot, (8, DK)))
            prep.append(dict(
                a16=a.astype(BF16), diag=_compress_diag_blocks(a), rhs=rhs,
                qkd=(qk[i] * dec_qk).astype(BF16),
                kdt=(k[i] * jnp.exp(tot - cum)).T.astype(BF16),
                qe=q[i] * e_col, b=b))
        tm = [x.astype(BF16) for x in _tri_inv_lower([p["diag"] for p in prep], [p["a16"] for p in prep])]
        uw = []
        for (i, d), t_, p in zip(items, tm, prep):
            if d == 0:
                uw.append(_dot(t_, p["rhs"]))
            else:
                uw.append(jnp.concatenate([p["b"], p["b"]], axis=1)
                          * lax.dot_general(t_, p["rhs"], (((0,), (0,)), ((), ())), preferred_element_type=F32))
        uwb = [x.astype(BF16) for x in uw]
        kuw = [_dot(p["kdt"], x) for p, x in zip(prep, uwb)]
        quw = [_dot(p["qkd"], x) for p, x in zip(prep, uwb)]
        for idx, (i, d) in enumerate(items):
            r_ref[d, pl.ds(r128[i], DK), :] = kuw[idx][:, 0:DV]
            mq_ref[d, pl.ds(r192[i], DK), :] = (-kuw[idx][:, DV:DV + DK]).astype(BF16)
            mq_ref[d, pl.ds(r192[i] + DK, c_len), :] = (prep[idx]["qe"] - quw[idx][:, DV:DV + DK]).astype(BF16)
            o_ref[d, pl.ds(r64[i], c_len), :] = quw[idx][:, 0:DV]
        return carry

    lax.fori_loop(0, n // nb, phase1, 0)

    def phase2(i, carry):
        new = []
        for d in range(2):
            s = carry[d]
            c = i if d == 0 else n - 1 - i
            r64 = pl.multiple_of(c * c_len, c_len)
            r128 = pl.multiple_of(c * DK, DK)
            r192 = pl.multiple_of(c * (DK + c_len), DK + c_len)
            ms = _dot(mq_ref[d, pl.ds(r192, DK + c_len), :], s.astype(BF16))
            et = et_ref[d, pl.ds(pl.multiple_of(c * 8, 8), 8), :]
            new.append(s * et[0:1, :] + r_ref[d, pl.ds(r128, DK), :] + ms[0:DK])
            o_ref[d, pl.ds(r64, c_len), :] = o_ref[d, pl.ds(r64, c_len), :] + ms[DK:DK + c_len]
        conv_block(i)
        return tuple(new)

    s_f, s_b = lax.fori_loop(0, n, phase2, (sf0_ref[...], sb0_ref[...]))
    sf_ref[...] = s_f
    sb_ref[...] = s_b

    for r in range(0, t, rb):
        o = o_ref[0, r:r + rb, :] + o_ref[1, r:r + rb, :]
        ms = jnp.mean(o * o, axis=-1, keepdims=True)
        y = o * lax.rsqrt(ms + EPS) * ng_ref[...]
        ya_ref[r:r + rb, :] = (y * _silu(z_ref[r:r + rb, :].astype(F32))).astype(BF16)


def _deltanet(qkv, z, ba, conv_w, norm_g, s_f0, s_b0):
    bsz, t, _ = qkv.shape
    n = t // DN_CHUNK
    n_seq = bsz * H_A
    prep = lambda s: jnp.minimum(s, n_seq - 1)
    run = lambda s: jnp.maximum(s - 1, 0)
    prep_col = lambda off: (lambda s: (prep(s) // H_A, 0, off + prep(s) % H_A))
    prep_cw = lambda off: (lambda s: (0, off + prep(s) % H_A))
    run_col = lambda s: (run(s) // H_A, 0, run(s) % H_A)
    run_state = lambda s: (run(s) // H_A, run(s) % H_A, 0, 0)
    return pl.pallas_call(
        functools.partial(_dn_kernel, seq=t),
        grid=(n_seq + 1,),
        in_specs=[
            pl.BlockSpec((None, t, DK), prep_col(0)),
            pl.BlockSpec((None, t, DK), prep_col(H_A)),
            pl.BlockSpec((None, t, DV), prep_col(2 * H_A)),
            pl.BlockSpec((None, t, DV), run_col),
            pl.BlockSpec((None, t, GATE_W), lambda s: (run(s) // H_A, 0, 0)),
            pl.BlockSpec((CONV_K, DK), prep_cw(0)),
            pl.BlockSpec((CONV_K, DK), prep_cw(H_A)),
            pl.BlockSpec((CONV_K, DV), prep_cw(2 * H_A)),
            pl.BlockSpec((1, DV), lambda s: (0, 0)),
            pl.BlockSpec((None, None, DK, DV), run_state),
            pl.BlockSpec((None, None, DK, DV), run_state),
        ],
        out_specs=[
            pl.BlockSpec((None, t, DV), run_col),
            pl.BlockSpec((None, None, DK, DV), run_state),
            pl.BlockSpec((None, None, DK, DV), run_state),
        ],
        out_shape=[
            jax.ShapeDtypeStruct((bsz, t, W_A), BF16),
            jax.ShapeDtypeStruct((bsz, H_A, DK, DV), F32),
            jax.ShapeDtypeStruct((bsz, H_A, DK, DV), F32),
        ],
        scratch_shapes=[
            pltpu.VMEM((t + 16, DK), F32),
            pltpu.VMEM((t + 16, DK), F32),
            pltpu.VMEM((t + 16, DV), F32),
            pltpu.VMEM((2, t, DK), F32),
            pltpu.VMEM((2, t, DK), F32),
            pltpu.VMEM((2, t, DV), F32),
            pltpu.VMEM((2, n * DK, DV), F32),
            pltpu.VMEM((2, n * (DK + DN_CHUNK), DK), BF16),
            pltpu.VMEM((2, n * 8, DV), F32),
            pltpu.VMEM((2, t, DV), F32),
        ],
        compiler_params=pltpu.CompilerParams(vmem_limit_bytes=VMEM_LIMIT),
        name="deltanet",
    )(qkv, qkv, qkv, z, ba, conv_w, conv_w, conv_w, norm_g, s_f0, s_b0)


def _outffn_kernel(x_ref, ya_ref, yb_ref, mod_ref, n2_ref, fg_ref, wo_ref, wg_ref, wu_ref, wd_ref, o_ref,
                   act_ref, *, tm):
    subs = [slice(r, r + FFN_SUB) for r in range(0, tm, FFN_SUB)]
    hb = []
    for r in subs:
        y = _dot(ya_ref[r, :], wo_ref[0:W_A, :]) + _dot(yb_ref[r, :], wo_ref[W_A:D_MODEL, :])
        x1 = x_ref[r, :] + mod_ref[2:3, :] * y
        o_ref[r, :] = x1
        ms = jnp.mean(x1 * x1, axis=-1, keepdims=True)
        h = (x1 * lax.rsqrt(ms + EPS) * n2_ref[...]) * (1.0 + mod_ref[4:5, :]) + mod_ref[3:4, :]
        hb.append(h.astype(BF16))
    for c in range(D_FF // FF_CHUNK):
        l = slice(c * FF_CHUNK, (c + 1) * FF_CHUNK)
        for r, h in zip(subs, hb):
            act_ref[r, l] = (_silu(_dot(h, wg_ref[:, l])) * _dot(h, wu_ref[:, l])).astype(BF16)
    for r in subs:
        x2 = o_ref[r, :] + mod_ref[5:6, :] * _dot(act_ref[r, :], wd_ref[...])
        ms2 = jnp.mean(x2 * x2, axis=-1, keepdims=True)
        o_ref[r, :] = x2 * lax.rsqrt(ms2 + EPS) * fg_ref[...]


def _outffn(x2d, ya, yb, mod, n2, fg, w_out, w_gate, w_up, w_down, *, seq, row0, row_stride, tm):
    n_tok = x2d.shape[0]
    assert row_stride == 0 or seq % tm == 0

    def mod_idx(i):
        return (row0 + ((i * tm) // seq) * row_stride, 0, 0)

    const2 = lambda i: (0, 0)
    tok = lambda i: (i, 0)
    resident = lambda w: pl.BlockSpec(w.shape, const2, pipeline_mode=pl.Buffered(1))
    return pl.pallas_call(
        functools.partial(_outffn_kernel, tm=tm),
        grid=(n_tok // tm,),
        in_specs=[
            pl.BlockSpec((tm, D_MODEL), tok),
            pl.BlockSpec((tm, W_A), tok),
            pl.BlockSpec((tm, W_B), tok),
            pl.BlockSpec((None, N_MOD, D_MODEL), mod_idx),
            pl.BlockSpec((1, D_MODEL), const2),
            pl.BlockSpec((1, D_MODEL), const2),
            resident(w_out), resident(w_gate), resident(w_up), resident(w_down),
        ],
        out_specs=pl.BlockSpec((tm, D_MODEL), tok),
        out_shape=jax.ShapeDtypeStruct((n_tok, D_MODEL), F32),
        scratch_shapes=[pltpu.VMEM((tm, D_FF), BF16)],
        compiler_params=pltpu.CompilerParams(vmem_limit_bytes=VMEM_LIMIT),
        name="outffn",
    )(x2d, ya, yb, mod, n2, fg, w_out, w_gate, w_up, w_down)


def _trunk_layer(x, mod, row0, row_stride, s_f0, s_b0, p, final_g):
    bsz, t, _ = x.shape
    x2d = x.reshape(bsz * t, D_MODEL)
    qkv, z, ba, yb = _inproj(x2d, mod, p["g1"], p["w_main"], p["w_ba"], p["a_row"], p["dt_row"], p["ln_g"], p["ln_b"], p["w_s"],
                             p["b_s_t"], seq=t, row0=row0, row_stride=row_stride, tm=IN_TM)
    ya, s_f, s_b = _deltanet(qkv.reshape(bsz, t, 3 * W_A), z.reshape(bsz, t, W_A), ba.reshape(bsz, t, GATE_W),
                             p["conv_w"], p["dn_g"], s_f0, s_b0)
    out = _outffn(x2d, ya.reshape(bsz * t, W_A), yb, mod, p["g2"], final_g, p["w_out"], p["w_gate"], p["w_up"],
                  p["w_down"], seq=t, row0=row0, row_stride=row_stride, tm=FFN_TM)
    return out.reshape(bsz, t, D_MODEL), s_f, s_b


def kernel(x_prompt, x_sample, state_fwd, state_bwd, c, c_ctx, w_ada, b_ada, norm1_g, norm2_g, w_in, conv_w,
           a_log, dt_bias, dn_norm_g, sgu_ln_g, sgu_ln_b, sgu_w, sgu_b, w_out, w_gate, w_up, w_down, final_g):
    depth = w_ada.shape[0]
    assert depth == 1, "the final RMSNorm is fused into the last layer; only depth 1 is supported"
    n_lat = c.shape[0]
    assert 1 + n_lat <= MOD_ROWS
    cc = jnp.zeros((MOD_ROWS, D_MODEL), F32).at[0].set(c_ctx).at[1:1 + n_lat].set(c)
    s_zero = jnp.zeros((x_prompt.shape[0], H_A, DK, DV), F32)
    xp, xs = x_prompt, x_sample
    new_f, new_b = [], []
    gate_pad = jnp.zeros((2 * H_A,), F32)
    for l in range(depth):
        w_in_b = w_in[l].astype(BF16)
        n_main = 4 * W_A + 2 * W_B
        p = dict(
            g1=norm1_g[l].reshape(1, D_MODEL), g2=norm2_g[l].reshape(1, D_MODEL),
            w_main=w_in_b[:, :n_main],
            w_ba=jnp.pad(w_in_b[:, n_main:], ((0, 0), (0, GATE_W - 4 * H_A))),
            ln_g=sgu_ln_g[l].reshape(1, W_B), ln_b=sgu_ln_b[l].reshape(1, W_B),
            w_s=sgu_w[l].astype(BF16), b_s_t=jnp.transpose(sgu_b[l]),
            conv_w=conv_w[l],
            a_row=jnp.pad(jnp.concatenate([gate_pad, a_log[l].reshape(-1)]), (0, GATE_W - 4 * H_A)).reshape(1, GATE_W),
            dt_row=jnp.pad(jnp.concatenate([gate_pad, dt_bias[l].reshape(-1)]), (0, GATE_W - 4 * H_A)).reshape(1, GATE_W),
            dn_g=dn_norm_g[l].reshape(1, DV),
            w_out=w_out[l].astype(BF16), w_gate=w_gate[l].astype(BF16), w_up=w_up[l].astype(BF16),
            w_down=w_down[l].astype(BF16),
        )
        mod = _ada(cc, w_ada[l], b_ada[l]).reshape(MOD_ROWS, N_MOD, D_MODEL)
        fg = final_g.reshape(1, D_MODEL)
        xp, sf, sb = _trunk_layer(xp, mod, 0, 0, s_zero, s_zero, p, fg)
        new_f.append(sf)
        new_b.append(sb)
        xs, _, _ = _trunk_layer(xs, mod, 1, 1, state_fwd[:, l], state_bwd[:, l], p, fg)
    return (xp, xs, jnp.stack(new_f, axis=1), jnp.stack(new_b, axis=1))
```

```python
import functools

import jax
import jax.numpy as jnp
from jax import lax
from jax.experimental import pallas as pl
from jax.experimental.pallas import tpu as pltpu

F32 = jnp.float32
BF16 = jnp.bfloat16

D_MODEL = 1024
W_A = D_MODEL // 2
DK = 128
DV = 128
H_A = W_A // DV
W_B = D_MODEL - W_A
WG = 128
G_B = W_B // WG
SGU_CHUNK = 128
CONV_K = 5
DN_CHUNK = 128
D_FF = 2816
N_MOD = 6
EPS = 1e-6

MOD_ROWS = 16
GATE_W = 128
LANES = 128
INV_BASE = 16
FF_CHUNK = 256
IN_SUB = 256
IN_TM = 1024
FFN_SUB = 256
FFN_TM = 1024
DN_SHORT = 512
P1_BATCH = 8
VMEM_LIMIT = 56 * 1024 * 1024


def _sigmoid(x):
    return 0.5 + 0.5 * jnp.tanh(0.5 * x)


def _silu(x):
    hx = 0.5 * x
    return hx + hx * jnp.tanh(hx)


def _gelu_tanh(x):
    c = 0.7978845608028654
    return 0.5 * x * (1.0 + jnp.tanh(c * (x + 0.044715 * (x * x * x))))


def _softplus(x):
    return jnp.maximum(x, 0.0) + jnp.log(1.0 + jnp.exp(-jnp.abs(x)))


def _dot(a, b):
    return jnp.dot(a, b, preferred_element_type=F32)


def _dot_nt(a, b):
    return lax.dot_general(a, b, (((1,), (1,)), ((), ())), preferred_element_type=F32)


def _split3(x):
    hi = x.astype(BF16)
    r1 = x - hi.astype(F32)
    mid = r1.astype(BF16)
    lo = (r1 - mid.astype(F32)).astype(BF16)
    return hi, mid, lo


def _dot_exact_lhs(a_bf16, x):
    hi, mid, lo = _split3(x)
    return _dot(a_bf16, hi) + _dot(a_bf16, mid) + _dot(a_bf16, lo)


def _ada_kernel(c_ref, w_ref, b_ref, o_ref):
    s = _silu(c_ref[...])
    o_ref[...] = _dot(s.astype(BF16), w_ref[...].astype(BF16)) + b_ref[...]


def _ada(cc, w_ada, b_ada):
    n_out = w_ada.shape[1]
    bn = D_MODEL
    return pl.pallas_call(
        _ada_kernel,
        grid=(n_out // bn,),
        in_specs=[
            pl.BlockSpec((MOD_ROWS, D_MODEL), lambda j: (0, 0)),
            pl.BlockSpec((D_MODEL, bn), lambda j: (0, j)),
            pl.BlockSpec((1, bn), lambda j: (0, j)),
        ],
        out_specs=pl.BlockSpec((MOD_ROWS, bn), lambda j: (0, j)),
        out_shape=jax.ShapeDtypeStruct((MOD_ROWS, n_out), F32),
        compiler_params=pltpu.CompilerParams(vmem_limit_bytes=VMEM_LIMIT),
        name="ada",
    )(cc, w_ada, b_ada.reshape(1, n_out))


def _inproj_kernel(x_ref, mod_ref, g1_ref, wm_ref, wba_ref, arow_ref, dtrow_ref, lng_ref, lnb_ref, ws_ref, bs_ref,
                   qkv_ref, z_ref, ba_ref, yb_ref, *, tm):
    subs = [slice(r, r + IN_SUB) for r in range(0, tm, IN_SUB)]
    ri = lax.broadcasted_iota(jnp.int32, (IN_SUB, IN_SUB), 0)
    ci = lax.broadcasted_iota(jnp.int32, (IN_SUB, IN_SUB), 1)
    same_chunk = (ri // DN_CHUNK) == (ci // DN_CHUNK)
    prefix_op = (same_chunk & (ri >= ci)).astype(BF16)
    suffix_op = (same_chunk & (ri <= ci)).astype(BF16)

    def norm(r):
        x = x_ref[r, :]
        ms = jnp.mean(x * x, axis=-1, keepdims=True)
        xn = x * lax.rsqrt(ms + EPS) * g1_ref[...]
        return (xn * (1.0 + mod_ref[1:2, :]) + mod_ref[0:1, :]).astype(BF16)

    def matmul_tasks(r, hb):
        def qkv_piece(j):
            def run():
                l = slice(j * W_A, (j + 1) * W_A)
                qkv_ref[r, l] = _dot(hb, wm_ref[:, l]).astype(BF16)
            return run

        def z_piece():
            z_ref[r, :] = _dot(hb, wm_ref[:, 3 * W_A:4 * W_A]).astype(BF16)

        def gate_piece():
            logit = _dot(hb, wba_ref[...])
            lane = lax.broadcasted_iota(jnp.int32, logit.shape, 1)
            g_all = -jnp.exp(arow_ref[...]) * _softplus(logit + dtrow_ref[...])
            cum_f = _dot_exact_lhs(prefix_op, g_all)
            cum_b = _dot_exact_lhs(suffix_op, g_all)
            ba_ref[r, :] = jnp.where(lane < 2 * H_A, _sigmoid(logit), jnp.where(lane < 3 * H_A, cum_f, cum_b))

        return [qkv_piece(0), qkv_piece(1), qkv_piece(2), z_piece, gate_piece]

    def sgu_tasks(r, st):
        def gelu_u():
            st["u"] = _gelu_tanh(st["u"])

        def gelu_v():
            st["v"] = _gelu_tanh(st["v"])

        def layernorm_v():
            vb = st["v"]
            mu = jnp.mean(vb, axis=-1, keepdims=True)
            vc = vb - mu
            var = jnp.mean(vc * vc, axis=-1, keepdims=True)
            st["v"] = (vc * lax.rsqrt(var + EPS) * lng_ref[...] + lnb_ref[...]).astype(BF16)

        def mix(c):
            def run():
                rr = slice(c * SGU_CHUNK, (c + 1) * SGU_CHUNK)
                ro = slice(r.start + c * SGU_CHUNK, r.start + (c + 1) * SGU_CHUNK)
                for g in range(G_B):
                    l = slice(g * WG, (g + 1) * WG)
                    mixed = _dot(ws_ref[g], st["v"][rr, l]) + bs_ref[:, g:g + 1]
                    yb_ref[ro, l] = (st["u"][rr, l] * mixed).astype(BF16)
            return run

        return [gelu_u, gelu_v, layernorm_v] + [mix(c) for c in range(IN_SUB // SGU_CHUNK)]

    hb = norm(subs[0])
    for i, r in enumerate(subs):
        st = {"u": _dot(hb, wm_ref[:, 4 * W_A:4 * W_A + W_B]),
              "v": _dot(hb, wm_ref[:, 4 * W_A + W_B:4 * W_A + 2 * W_B])}
        mm = matmul_tasks(r, hb)
        ew = sgu_tasks(r, st)
        nxt = {}
        if i + 1 < len(subs):
            ew.append(lambda i=i: nxt.update(hb=norm(subs[i + 1])))
        for k in range(max(len(mm), len(ew))):
            if k < len(mm):
                mm[k]()
            if k < len(ew):
                ew[k]()
        hb = nxt.get("hb")


def _inproj(x2d, mod, g1, w_main, w_ba, a_row, dt_row, ln_g, ln_b, w_s, b_s_t, *, seq, row0, row_stride, tm):
    n_tok = x2d.shape[0]
    assert row_stride == 0 or seq % tm == 0

    def mod_idx(i):
        return (row0 + ((i * tm) // seq) * row_stride, 0, 0)

    const2 = lambda i: (0, 0)
    tok = lambda i: (i, 0)
    return pl.pallas_call(
        functools.partial(_inproj_kernel, tm=tm),
        grid=(n_tok // tm,),
        in_specs=[
            pl.BlockSpec((tm, D_MODEL), tok),
            pl.BlockSpec((None, N_MOD, D_MODEL), mod_idx),
            pl.BlockSpec((1, D_MODEL), const2),
            pl.BlockSpec(w_main.shape, const2, pipeline_mode=pl.Buffered(1)),
            pl.BlockSpec(w_ba.shape, const2),
            pl.BlockSpec((1, GATE_W), const2),
            pl.BlockSpec((1, GATE_W), const2),
            pl.BlockSpec((1, W_B), const2),
            pl.BlockSpec((1, W_B), const2),
            pl.BlockSpec(w_s.shape, lambda i: (0, 0, 0)),
            pl.BlockSpec(b_s_t.shape, const2),
        ],
        out_specs=[
            pl.BlockSpec((tm, 3 * W_A), tok),
            pl.BlockSpec((tm, W_A), tok),
            pl.BlockSpec((tm, GATE_W), tok),
            pl.BlockSpec((tm, W_B), tok),
        ],
        out_shape=[
            jax.ShapeDtypeStruct((n_tok, 3 * W_A), BF16),
            jax.ShapeDtypeStruct((n_tok, W_A), BF16),
            jax.ShapeDtypeStruct((n_tok, GATE_W), F32),
            jax.ShapeDtypeStruct((n_tok, W_B), BF16),
        ],
        compiler_params=pltpu.CompilerParams(vmem_limit_bytes=VMEM_LIMIT),
        name="inproj",
    )(x2d, mod, g1, w_main, w_ba, a_row, dt_row, ln_g, ln_b, w_s, b_s_t)


def _compress_diag_blocks(a):
    n = a.shape[0]
    lane = lax.broadcasted_iota(jnp.int32, (INV_BASE, n), 1)
    out = jnp.where((lane // INV_BASE) == 0, a[0:INV_BASE, :], 0.0)
    for rb in range(1, n // INV_BASE):
        out = out + jnp.where((lane // INV_BASE) == rb, a[rb * INV_BASE:(rb + 1) * INV_BASE, :], 0.0)
    return out


def _tri_inv_lower(diag_blocks, mats_bf16):
    n = DN_CHUNK
    nblk = n // INV_BASE
    per = LANES // n
    assert per >= 1 and len(diag_blocks) % per == 0
    lane1 = lax.broadcasted_iota(jnp.int32, (INV_BASE, n), 1)
    lane2 = lax.broadcasted_iota(jnp.int32, (INV_BASE, per * n), 1)
    sub2 = lax.broadcasted_iota(jnp.int32, (INV_BASE, per * n), 0)
    packed = [diag_blocks[p] if per == 1 else
              jnp.concatenate([diag_blocks[per * p + q] for q in range(per)], axis=1)
              for p in range(len(diag_blocks) // per)]
    eye = ((lane2 % INV_BASE) == sub2).astype(F32)
    xs = [eye for _ in packed]
    grp = (lane2 // INV_BASE) * INV_BASE
    for j in range(INV_BASE - 1):
        cols = [jnp.take_along_axis(nc, grp + j, axis=1) for nc in packed]
        xs = [x - c * x[j:j + 1, :] for x, c in zip(xs, cols)]

    def expand(xc):
        return jnp.concatenate([jnp.where((lane1 // INV_BASE) == rb, xc, 0.0) for rb in range(nblk)], axis=0)

    ds = []
    for x in xs:
        for q in range(per):
            ds.append(expand(x[:, q * n:(q + 1) * n]))
    row = lax.broadcasted_iota(jnp.int32, (n, n), 0)
    col = lax.broadcasted_iota(jnp.int32, (n, n), 1)
    zero = jnp.zeros((n, n), BF16)
    bs = INV_BASE
    while bs < n:
        cmask = ((row // (2 * bs)) == (col // (2 * bs))) & (((row // bs) % 2) == 1) & (((col // bs) % 2) == 0)
        cs = [jnp.where(cmask, a, zero) for a in mats_bf16]
        dbs = [d.astype(BF16) for d in ds]
        t1 = [_dot(db, c) for db, c in zip(dbs, cs)]
        t2 = [_dot(t.astype(BF16), db) for t, db in zip(t1, dbs)]
        ds = [d - t for d, t in zip(ds, t2)]
        bs *= 2
    return ds


def _dn_kernel(q_ref, k_ref, v_ref, z_ref, ba_ref, cwq_ref, cwk_ref, cwv_ref,
               ng_ref, sf0_ref, sb0_ref,
               ya_ref, sf_ref, sb_ref,
               padq_ref, padk_ref, padv_ref, qs_ref, ks_ref, vs_ref,
               r_ref, mq_ref, et_ref, o_ref, *, seq, hps):
    t = seq
    c_len = DN_CHUNK
    n = t // c_len
    step = pl.program_id(0)
    head0 = (jnp.maximum(step - 1, 0) % (H_A // hps)) * hps
    slot_new = step % 2
    slot_cur = 1 - slot_new
    rb = min(t, 256)
    heads = [slice(g * DK, (g + 1) * DK) for g in range(hps)]

    zeros8 = jnp.zeros((8, DK), F32)
    for src_ref, pad_ref in ((q_ref, padq_ref), (k_ref, padk_ref), (v_ref, padv_ref)):
        for g, l in enumerate(heads):
            pad_ref[g, 0:8, :] = zeros8
            pad_ref[g, 8 + t:16 + t, :] = zeros8
            for r in range(0, t, rb):
                pad_ref[g, 8 + r:8 + r + rb, :] = src_ref[r:r + rb, l].astype(F32)

    def conv_block(i):
        r0 = pl.multiple_of(i * c_len, c_len)
        for pad_ref, cw_ref, dst_ref, l2, scale in ((padq_ref, cwq_ref, qs_ref, True, DK ** -0.5),
                                                    (padk_ref, cwk_ref, ks_ref, True, 1.0),
                                                    (padv_ref, cwv_ref, vs_ref, False, 1.0)):
            for g, l in enumerate(heads):
                acc = cw_ref[0:1, l] * pad_ref[g, pl.ds(r0 + 6, c_len), :]
                for j in range(1, CONV_K):
                    acc = acc + cw_ref[j:j + 1, l] * pad_ref[g, pl.ds(r0 + 6 + j, c_len), :]
                y = _silu(acc)
                if l2:
                    y = y * (lax.rsqrt(jnp.sum(y * y, axis=-1, keepdims=True) + EPS) * scale)
                dst_ref[slot_new, g, pl.ds(r0, c_len), :] = y

    @pl.when(step == 0)
    def _():
        def body(i, carry):
            conv_block(i)
            return carry
        lax.fori_loop(0, n, body, 0)

    @pl.when(step > 0)
    def _():
        _dn_group(z_ref, ba_ref, ng_ref, sf0_ref, sb0_ref, ya_ref, sf_ref, sb_ref,
                  qs_ref.at[slot_cur], ks_ref.at[slot_cur], vs_ref.at[slot_cur],
                  r_ref, mq_ref, et_ref, o_ref, conv_block, head0, t, hps)


def _dn_group(z_ref, ba_ref, ng_ref, sf0_ref, sb0_ref, ya_ref, sf_ref, sb_ref, qs_ref, ks_ref, vs_ref,
              r_ref, mq_ref, et_ref, o_ref, conv_block, head0, t, hps):
    c_len = DN_CHUNK
    n = t // c_len
    rb = min(t, 256)

    row = lax.broadcasted_iota(jnp.int32, (c_len, c_len), 0)
    col = lax.broadcasted_iota(jnp.int32, (c_len, c_len), 1)
    lower_incl = row >= col
    upper_incl = row <= col
    lower_strict = row > col
    gate_lane = [jnp.full((c_len, GATE_W), head0 + g, jnp.int32) for g in range(hps)]
    nb = max(1, min(P1_BATCH // hps, n))
    assert n % nb == 0

    def phase1(it, carry):
        cs = [it * nb + i for i in range(nb)]
        r64 = [pl.multiple_of(c * c_len, c_len) for c in cs]
        r128 = [pl.multiple_of(c * DK, DK) for c in cs]
        r192 = [pl.multiple_of(c * (DK + c_len), DK + c_len) for c in cs]
        r8 = [pl.multiple_of(c * 8, 8) for c in cs]
        chunks = [(g, i) for g in range(hps) for i in range(nb)]
        q = {gi: qs_ref[gi[0], pl.ds(r64[gi[1]], c_len), :] for gi in chunks}
        k = {gi: ks_ref[gi[0], pl.ds(r64[gi[1]], c_len), :] for gi in chunks}
        kb = {gi: k[gi].astype(BF16) for gi in chunks}
        kk = {gi: _dot_nt(kb[gi], kb[gi]) for gi in chunks}
        qk = {gi: _dot_nt(q[gi].astype(BF16), kb[gi]) for gi in chunks}
        items = [(g, i, d) for g, i in chunks for d in range(2)]
        prep = []
        for g, i, d in items:
            gates = ba_ref[pl.ds(r64[i], c_len), :]
            b = jnp.take_along_axis(gates, gate_lane[g] + d * H_A, axis=1)
            cum = jnp.take_along_axis(gates, gate_lane[g] + (2 + d) * H_A, axis=1)
            dm = cum[:, 0:c_len] - cum[:, 0:c_len].T
            dec_sys = jnp.where(lower_incl, jnp.exp(dm if d == 0 else -dm), 0.0)
            dec_qk = dec_sys if d == 0 else jnp.where(upper_incl, jnp.exp(dm), 0.0)
            a = jnp.where(lower_strict, b[:, 0:c_len] * kk[g, i] * dec_sys, 0.0)
            e_col = jnp.exp(cum)
            v = vs_ref[g, pl.ds(r64[i], c_len), :]
            if d == 0:
                rhs = jnp.concatenate([v * b, k[g, i] * (b * e_col)], axis=1).astype(BF16)
            else:
                rhs = jnp.concatenate([v, k[g, i] * e_col], axis=1).astype(BF16)
            tot = cum[c_len - 1:c_len, :] if d == 0 else cum[0:1, :]
            et_ref[g, d, pl.ds(r8[i], 8), :] = jnp.exp(jnp.broadcast_to(tot, (8, DK)))
            prep.append(dict(
                a16=a.astype(BF16), diag=_compress_diag_blocks(a), rhs=rhs,
                qkd=(qk[g, i] * dec_qk).astype(BF16),
                kdt=(k[g, i] * jnp.exp(tot - cum)).T.astype(BF16),
                qe=q[g, i] * e_col, b=b))
        tm = [x.astype(BF16) for x in _tri_inv_lower([p["diag"] for p in prep], [p["a16"] for p in prep])]
        uw = []
        for (g, i, d), t_, p in zip(items, tm, prep):
            if d == 0:
                uw.append(_dot(t_, p["rhs"]))
            else:
                uw.append(jnp.concatenate([p["b"], p["b"]], axis=1)
                          * lax.dot_general(t_, p["rhs"], (((0,), (0,)), ((), ())), preferred_element_type=F32))
        uwb = [x.astype(BF16) for x in uw]
        kuw = [_dot(p["kdt"], x) for p, x in zip(prep, uwb)]
        quw = [_dot(p["qkd"], x) for p, x in zip(prep, uwb)]
        for idx, (g, i, d) in enumerate(items):
            r_ref[g, d, pl.ds(r128[i], DK), :] = kuw[idx][:, 0:DV]
            mq_ref[g, d, pl.ds(r192[i], DK), :] = (-kuw[idx][:, DV:DV + DK]).astype(BF16)
            mq_ref[g, d, pl.ds(r192[i] + DK, c_len), :] = (prep[idx]["qe"] - quw[idx][:, DV:DV + DK]).astype(BF16)
            o_ref[g, d, pl.ds(r64[i], c_len), :] = quw[idx][:, 0:DV]
        return carry

    lax.fori_loop(0, n // nb, phase1, 0)

    def phase2(i, carry):
        new = []
        for g in range(hps):
            for d in range(2):
                s = carry[2 * g + d]
                c = i if d == 0 else n - 1 - i
                r64 = pl.multiple_of(c * c_len, c_len)
                r128 = pl.multiple_of(c * DK, DK)
                r192 = pl.multiple_of(c * (DK + c_len), DK + c_len)
                ms = _dot(mq_ref[g, d, pl.ds(r192, DK + c_len), :], s.astype(BF16))
                et = et_ref[g, d, pl.ds(pl.multiple_of(c * 8, 8), 8), :]
                new.append(s * et[0:1, :] + r_ref[g, d, pl.ds(r128, DK), :] + ms[0:DK])
                o_ref[g, d, pl.ds(r64, c_len), :] = o_ref[g, d, pl.ds(r64, c_len), :] + ms[DK:DK + c_len]
        conv_block(i)
        return tuple(new)

    init = []
    for g in range(hps):
        init += [sf0_ref[g], sb0_ref[g]]
    fin = lax.fori_loop(0, n, phase2, tuple(init))
    for g in range(hps):
        sf_ref[g] = fin[2 * g]
        sb_ref[g] = fin[2 * g + 1]

    for g in range(hps):
        l = slice(g * DV, (g + 1) * DV)
        for r in range(0, t, rb):
            o = o_ref[g, 0, r:r + rb, :] + o_ref[g, 1, r:r + rb, :]
            ms = jnp.mean(o * o, axis=-1, keepdims=True)
            y = o * lax.rsqrt(ms + EPS) * ng_ref[...]
            ya_ref[r:r + rb, l] = (y * _silu(z_ref[r:r + rb, l].astype(F32))).astype(BF16)


def _deltanet(qkv, z, ba, conv_w, norm_g, s_f0, s_b0, *, hps):
    bsz, t, _ = qkv.shape
    n = t // DN_CHUNK
    hb = H_A // hps
    n_grp = bsz * hb
    prep = lambda s: jnp.minimum(s, n_grp - 1)
    run = lambda s: jnp.maximum(s - 1, 0)
    prep_col = lambda off: (lambda s: (prep(s) // hb, 0, off + prep(s) % hb))
    prep_cw = lambda off: (lambda s: (0, off + prep(s) % hb))
    run_col = lambda s: (run(s) // hb, 0, run(s) % hb)
    run_state = lambda s: (run(s) // hb, run(s) % hb, 0, 0)
    w = hps * DK
    return pl.pallas_call(
        functools.partial(_dn_kernel, seq=t, hps=hps),
        grid=(n_grp + 1,),
        in_specs=[
            pl.BlockSpec((None, t, w), prep_col(0)),
            pl.BlockSpec((None, t, w), prep_col(hb)),
            pl.BlockSpec((None, t, w), prep_col(2 * hb)),
            pl.BlockSpec((None, t, w), run_col),
            pl.BlockSpec((None, t, GATE_W), lambda s: (run(s) // hb, 0, 0)),
            pl.BlockSpec((CONV_K, w), prep_cw(0)),
            pl.BlockSpec((CONV_K, w), prep_cw(hb)),
            pl.BlockSpec((CONV_K, w), prep_cw(2 * hb)),
            pl.BlockSpec((1, DV), lambda s: (0, 0)),
            pl.BlockSpec((None, hps, DK, DV), run_state),
            pl.BlockSpec((None, hps, DK, DV), run_state),
        ],
        out_specs=[
            pl.BlockSpec((None, t, w), run_col),
            pl.BlockSpec((None, hps, DK, DV), run_state),
            pl.BlockSpec((None, hps, DK, DV), run_state),
        ],
        out_shape=[
            jax.ShapeDtypeStruct((bsz, t, W_A), BF16),
            jax.ShapeDtypeStruct((bsz, H_A, DK, DV), F32),
            jax.ShapeDtypeStruct((bsz, H_A, DK, DV), F32),
        ],
        scratch_shapes=[
            pltpu.VMEM((hps, t + 16, DK), F32),
            pltpu.VMEM((hps, t + 16, DK), F32),
            pltpu.VMEM((hps, t + 16, DV), F32),
            pltpu.VMEM((2, hps, t, DK), F32),
            pltpu.VMEM((2, hps, t, DK), F32),
            pltpu.VMEM((2, hps, t, DV), F32),
            pltpu.VMEM((hps, 2, n * DK, DV), F32),
            pltpu.VMEM((hps, 2, n * (DK + DN_CHUNK), DK), BF16),
            pltpu.VMEM((hps, 2, n * 8, DV), F32),
            pltpu.VMEM((hps, 2, t, DV), F32),
        ],
        compiler_params=pltpu.CompilerParams(vmem_limit_bytes=VMEM_LIMIT),
        name="deltanet",
    )(qkv, qkv, qkv, z, ba, conv_w, conv_w, conv_w, norm_g, s_f0, s_b0)


def _outffn_kernel(x_ref, ya_ref, yb_ref, mod_ref, n2_ref, fg_ref, wo_ref, wg_ref, wu_ref, wd_ref, o_ref,
                   act_ref, *, tm):
    subs = [slice(r, r + FFN_SUB) for r in range(0, tm, FFN_SUB)]
    hb = []
    for r in subs:
        y = _dot(ya_ref[r, :], wo_ref[0:W_A, :]) + _dot(yb_ref[r, :], wo_ref[W_A:D_MODEL, :])
        x1 = x_ref[r, :] + mod_ref[2:3, :] * y
        o_ref[r, :] = x1
        ms = jnp.mean(x1 * x1, axis=-1, keepdims=True)
        h = (x1 * lax.rsqrt(ms + EPS) * n2_ref[...]) * (1.0 + mod_ref[4:5, :]) + mod_ref[3:4, :]
        hb.append(h.astype(BF16))
    for c in range(D_FF // FF_CHUNK):
        l = slice(c * FF_CHUNK, (c + 1) * FF_CHUNK)
        for r, h in zip(subs, hb):
            act_ref[r, l] = (_silu(_dot(h, wg_ref[:, l])) * _dot(h, wu_ref[:, l])).astype(BF16)
    for r in subs:
        x2 = o_ref[r, :] + mod_ref[5:6, :] * _dot(act_ref[r, :], wd_ref[...])
        ms2 = jnp.mean(x2 * x2, axis=-1, keepdims=True)
        o_ref[r, :] = x2 * lax.rsqrt(ms2 + EPS) * fg_ref[...]


def _outffn(x2d, ya, yb, mod, n2, fg, w_out, w_gate, w_up, w_down, *, seq, row0, row_stride, tm):
    n_tok = x2d.shape[0]
    assert row_stride == 0 or seq % tm == 0

    def mod_idx(i):
        return (row0 + ((i * tm) // seq) * row_stride, 0, 0)

    const2 = lambda i: (0, 0)
    tok = lambda i: (i, 0)
    resident = lambda w: pl.BlockSpec(w.shape, const2, pipeline_mode=pl.Buffered(1))
    return pl.pallas_call(
        functools.partial(_outffn_kernel, tm=tm),
        grid=(n_tok // tm,),
        in_specs=[
            pl.BlockSpec((tm, D_MODEL), tok),
            pl.BlockSpec((tm, W_A), tok),
            pl.BlockSpec((tm, W_B), tok),
            pl.BlockSpec((None, N_MOD, D_MODEL), mod_idx),
            pl.BlockSpec((1, D_MODEL), const2),
            pl.BlockSpec((1, D_MODEL), const2),
            resident(w_out), resident(w_gate), resident(w_up), resident(w_down),
        ],
        out_specs=pl.BlockSpec((tm, D_MODEL), tok),
        out_shape=jax.ShapeDtypeStruct((n_tok, D_MODEL), F32),
        scratch_shapes=[pltpu.VMEM((tm, D_FF), BF16)],
        compiler_params=pltpu.CompilerParams(vmem_limit_bytes=VMEM_LIMIT),
        name="outffn",
    )(x2d, ya, yb, mod, n2, fg, w_out, w_gate, w_up, w_down)


def _trunk_layer(x, mod, row0, row_stride, s_f0, s_b0, p, final_g):
    bsz, t, _ = x.shape
    x2d = x.reshape(bsz * t, D_MODEL)
    qkv, z, ba, yb = _inproj(x2d, mod, p["g1"], p["w_main"], p["w_ba"], p["a_row"], p["dt_row"], p["ln_g"], p["ln_b"], p["w_s"],
                             p["b_s_t"], seq=t, row0=row0, row_stride=row_stride, tm=IN_TM)
    ya, s_f, s_b = _deltanet(qkv.reshape(bsz, t, 3 * W_A), z.reshape(bsz, t, W_A), ba.reshape(bsz, t, GATE_W),
                             p["conv_w"], p["dn_g"], s_f0, s_b0, hps=H_A if t <= DN_SHORT else 1)
    out = _outffn(x2d, ya.reshape(bsz * t, W_A), yb, mod, p["g2"], final_g, p["w_out"], p["w_gate"], p["w_up"],
                  p["w_down"], seq=t, row0=row0, row_stride=row_stride, tm=FFN_TM)
    return out.reshape(bsz, t, D_MODEL), s_f, s_b


def kernel(x_prompt, x_sample, state_fwd, state_bwd, c, c_ctx, w_ada, b_ada, norm1_g, norm2_g, w_in, conv_w,
           a_log, dt_bias, dn_norm_g, sgu_ln_g, sgu_ln_b, sgu_w, sgu_b, w_out, w_gate, w_up, w_down, final_g):
    depth = w_ada.shape[0]
    assert depth == 1, "the final RMSNorm is fused into the last layer; only depth 1 is supported"
    n_lat = c.shape[0]
    assert 1 + n_lat <= MOD_ROWS
    cc = jnp.zeros((MOD_ROWS, D_MODEL), F32).at[0].set(c_ctx).at[1:1 + n_lat].set(c)
    s_zero = jnp.zeros((x_prompt.shape[0], H_A, DK, DV), F32)
    xp, xs = x_prompt, x_sample
    new_f, new_b = [], []
    gate_pad = jnp.zeros((2 * H_A,), F32)
    for l in range(depth):
        w_in_b = w_in[l].astype(BF16)
        n_main = 4 * W_A + 2 * W_B
        p = dict(
            g1=norm1_g[l].reshape(1, D_MODEL), g2=norm2_g[l].reshape(1, D_MODEL),
            w_main=w_in_b[:, :n_main],
            w_ba=jnp.pad(w_in_b[:, n_main:], ((0, 0), (0, GATE_W - 4 * H_A))),
            ln_g=sgu_ln_g[l].reshape(1, W_B), ln_b=sgu_ln_b[l].reshape(1, W_B),
            w_s=sgu_w[l].astype(BF16), b_s_t=jnp.transpose(sgu_b[l]),
            conv_w=conv_w[l],
            a_row=jnp.pad(jnp.concatenate([gate_pad, a_log[l].reshape(-1)]), (0, GATE_W - 4 * H_A)).reshape(1, GATE_W),
            dt_row=jnp.pad(jnp.concatenate([gate_pad, dt_bias[l].reshape(-1)]), (0, GATE_W - 4 * H_A)).reshape(1, GATE_W),
            dn_g=dn_norm_g[l].reshape(1, DV),
            w_out=w_out[l].astype(BF16), w_gate=w_gate[l].astype(BF16), w_up=w_up[l].astype(BF16),
            w_down=w_down[l].astype(BF16),
        )
        mod = _ada(cc, w_ada[l], b_ada[l]).reshape(MOD_ROWS, N_MOD, D_MODEL)
        fg = final_g.reshape(1, D_MODEL)
        xp, sf, sb = _trunk_layer(xp, mod, 0, 0, s_zero, s_zero, p, fg)
        new_f.append(sf)
        new_b.append(sb)
        xs, _, _ = _trunk_layer(xs, mod, 1, 1, state_fwd[:, l], state_bwd[:, l], p, fg)
    return (xp, xs, jnp.stack(new_f, axis=1), jnp.stack(new_b, axis=1))
```

```python
import functools

import jax
import jax.numpy as jnp
from jax import lax
from jax.experimental import pallas as pl
from jax.experimental.pallas import tpu as pltpu

F32 = jnp.float32
BF16 = jnp.bfloat16

D_MODEL = 1024
W_A = D_MODEL // 2
DK = 128
DV = 128
H_A = W_A // DV
W_B = D_MODEL - W_A
WG = 128
G_B = W_B // WG
SGU_CHUNK = 128
CONV_K = 5
DN_CHUNK = 128
D_FF = 2816
N_MOD = 6
EPS = 1e-6

MOD_ROWS = 16
GATE_W = 128
LANES = 128
INV_BASE = 16
FF_CHUNK = 256
IN_SUB = 256
IN_TM = 1024
FFN_SUB = 256
FFN_TM = 1024
DN_SHORT = 512
P1_BATCH = 8
VMEM_LIMIT = 56 * 1024 * 1024


def _sigmoid(x):
    return 0.5 + 0.5 * jnp.tanh(0.5 * x)


def _silu(x):
    hx = 0.5 * x
    return hx + hx * jnp.tanh(hx)


def _gelu_tanh(x):
    c = 0.7978845608028654
    return 0.5 * x * (1.0 + jnp.tanh(c * (x + 0.044715 * (x * x * x))))


def _softplus(x):
    return jnp.maximum(x, 0.0) + jnp.log(1.0 + jnp.exp(-jnp.abs(x)))


def _dot(a, b):
    return jnp.dot(a, b, preferred_element_type=F32)


def _dot_nt(a, b):
    return lax.dot_general(a, b, (((1,), (1,)), ((), ())), preferred_element_type=F32)


def _dot_tn(a, b):
    return lax.dot_general(a, b, (((0,), (0,)), ((), ())), preferred_element_type=F32)


def _split3(x):
    hi = x.astype(BF16)
    r1 = x - hi.astype(F32)
    mid = r1.astype(BF16)
    lo = (r1 - mid.astype(F32)).astype(BF16)
    return hi, mid, lo


def _dot_exact_lhs(a_bf16, x):
    hi, mid, lo = _split3(x)
    return _dot(a_bf16, hi) + _dot(a_bf16, mid) + _dot(a_bf16, lo)


def _ada_kernel(c_ref, w_ref, b_ref, o_ref):
    s = _silu(c_ref[...])
    o_ref[...] = _dot(s.astype(BF16), w_ref[...].astype(BF16)) + b_ref[...]


def _ada(cc, w_ada, b_ada):
    n_out = w_ada.shape[1]
    bn = D_MODEL
    return pl.pallas_call(
        _ada_kernel,
        grid=(n_out // bn,),
        in_specs=[
            pl.BlockSpec((MOD_ROWS, D_MODEL), lambda j: (0, 0)),
            pl.BlockSpec((D_MODEL, bn), lambda j: (0, j)),
            pl.BlockSpec((1, bn), lambda j: (0, j)),
        ],
        out_specs=pl.BlockSpec((MOD_ROWS, bn), lambda j: (0, j)),
        out_shape=jax.ShapeDtypeStruct((MOD_ROWS, n_out), F32),
        compiler_params=pltpu.CompilerParams(vmem_limit_bytes=VMEM_LIMIT),
        name="ada",
    )(cc, w_ada, b_ada.reshape(1, n_out))


def _inproj_kernel(x_ref, mod_ref, g1_ref, wm_ref, wba_ref, arow_ref, dtrow_ref, lng_ref, lnb_ref, ws_ref, bs_ref,
                   qkv_ref, z_ref, ba_ref, yb_ref, *, tm):
    subs = [slice(r, r + IN_SUB) for r in range(0, tm, IN_SUB)]
    ri = lax.broadcasted_iota(jnp.int32, (IN_SUB, IN_SUB), 0)
    ci = lax.broadcasted_iota(jnp.int32, (IN_SUB, IN_SUB), 1)
    same_chunk = (ri // DN_CHUNK) == (ci // DN_CHUNK)
    prefix_op = (same_chunk & (ri >= ci)).astype(BF16)

    def norm(r):
        x = x_ref[r, :]
        ms = jnp.mean(x * x, axis=-1, keepdims=True)
        xn = x * lax.rsqrt(ms + EPS) * g1_ref[...]
        return (xn * (1.0 + mod_ref[1:2, :]) + mod_ref[0:1, :]).astype(BF16)

    def matmul_tasks(r, hb):
        def qkv_piece(j):
            def run():
                l = slice(j * W_A, (j + 1) * W_A)
                qkv_ref[r, l] = _dot(hb, wm_ref[:, l]).astype(BF16)
            return run

        def z_piece():
            z_ref[r, :] = _dot(hb, wm_ref[:, 3 * W_A:4 * W_A]).astype(BF16)

        def gate_piece():
            logit = _dot(hb, wba_ref[...])
            lane = lax.broadcasted_iota(jnp.int32, logit.shape, 1)
            g_all = -jnp.exp(arow_ref[...]) * _softplus(logit + dtrow_ref[...])
            cum_f = _dot_exact_lhs(prefix_op, g_all)
            tot = jnp.concatenate(
                [jnp.broadcast_to(cum_f[c + DN_CHUNK - 1:c + DN_CHUNK, :], (DN_CHUNK, GATE_W))
                 for c in range(0, IN_SUB, DN_CHUNK)], axis=0)
            cum_b = tot - cum_f + g_all
            ba_ref[r, :] = jnp.where(lane < 2 * H_A, _sigmoid(logit), jnp.where(lane < 3 * H_A, cum_f, cum_b))

        return [qkv_piece(0), qkv_piece(1), qkv_piece(2), z_piece, gate_piece]

    def sgu_tasks(r, st):
        def gelu_u():
            st["u"] = _gelu_tanh(st["u"])

        def gelu_v():
            st["v"] = _gelu_tanh(st["v"])

        def layernorm_v():
            vb = st["v"]
            mu = jnp.mean(vb, axis=-1, keepdims=True)
            vc = vb - mu
            var = jnp.mean(vc * vc, axis=-1, keepdims=True)
            st["v"] = (vc * lax.rsqrt(var + EPS) * lng_ref[...] + lnb_ref[...]).astype(BF16)

        def mix(c):
            def run():
                rr = slice(c * SGU_CHUNK, (c + 1) * SGU_CHUNK)
                ro = slice(r.start + c * SGU_CHUNK, r.start + (c + 1) * SGU_CHUNK)
                for g in range(G_B):
                    l = slice(g * WG, (g + 1) * WG)
                    mixed = _dot(ws_ref[g], st["v"][rr, l]) + bs_ref[:, g:g + 1]
                    yb_ref[ro, l] = (st["u"][rr, l] * mixed).astype(BF16)
            return run

        return [gelu_u, gelu_v, layernorm_v] + [mix(c) for c in range(IN_SUB // SGU_CHUNK)]

    hb = norm(subs[0])
    for i, r in enumerate(subs):
        st = {"u": _dot(hb, wm_ref[:, 4 * W_A:4 * W_A + W_B]),
              "v": _dot(hb, wm_ref[:, 4 * W_A + W_B:4 * W_A + 2 * W_B])}
        mm = matmul_tasks(r, hb)
        ew = sgu_tasks(r, st)
        nxt = {}
        if i + 1 < len(subs):
            ew.append(lambda i=i: nxt.update(hb=norm(subs[i + 1])))
        for k in range(max(len(mm), len(ew))):
            if k < len(mm):
                mm[k]()
            if k < len(ew):
                ew[k]()
        hb = nxt.get("hb")


def _inproj(x2d, mod, g1, w_main, w_ba, a_row, dt_row, ln_g, ln_b, w_s, b_s_t, *, seq, row0, row_stride, tm):
    n_tok = x2d.shape[0]
    assert row_stride == 0 or seq % tm == 0

    def mod_idx(i):
        return (row0 + ((i * tm) // seq) * row_stride, 0, 0)

    const2 = lambda i: (0, 0)
    tok = lambda i: (i, 0)
    return pl.pallas_call(
        functools.partial(_inproj_kernel, tm=tm),
        grid=(n_tok // tm,),
        in_specs=[
            pl.BlockSpec((tm, D_MODEL), tok),
            pl.BlockSpec((None, N_MOD, D_MODEL), mod_idx),
            pl.BlockSpec((1, D_MODEL), const2),
            pl.BlockSpec(w_main.shape, const2, pipeline_mode=pl.Buffered(1)),
            pl.BlockSpec(w_ba.shape, const2),
            pl.BlockSpec((1, GATE_W), const2),
            pl.BlockSpec((1, GATE_W), const2),
            pl.BlockSpec((1, W_B), const2),
            pl.BlockSpec((1, W_B), const2),
            pl.BlockSpec(w_s.shape, lambda i: (0, 0, 0)),
            pl.BlockSpec(b_s_t.shape, const2),
        ],
        out_specs=[
            pl.BlockSpec((tm, 3 * W_A), tok),
            pl.BlockSpec((tm, W_A), tok),
            pl.BlockSpec((tm, GATE_W), tok),
            pl.BlockSpec((tm, W_B), tok),
        ],
        out_shape=[
            jax.ShapeDtypeStruct((n_tok, 3 * W_A), BF16),
            jax.ShapeDtypeStruct((n_tok, W_A), BF16),
            jax.ShapeDtypeStruct((n_tok, GATE_W), F32),
            jax.ShapeDtypeStruct((n_tok, W_B), BF16),
        ],
        compiler_params=pltpu.CompilerParams(vmem_limit_bytes=VMEM_LIMIT),
        name="inproj",
    )(x2d, mod, g1, w_main, w_ba, a_row, dt_row, ln_g, ln_b, w_s, b_s_t)


def _compress_diag_blocks(a):
    n = a.shape[0]
    lane = lax.broadcasted_iota(jnp.int32, (INV_BASE, n), 1)
    out = jnp.where((lane // INV_BASE) == 0, a[0:INV_BASE, :], 0.0)
    for rb in range(1, n // INV_BASE):
        out = out + jnp.where((lane // INV_BASE) == rb, a[rb * INV_BASE:(rb + 1) * INV_BASE, :], 0.0)
    return out


def _tri_inv_stages(get_diag, get_mats, out):
    n = DN_CHUNK
    nblk = n // INV_BASE
    per = LANES // n
    assert per >= 1

    def eliminate():
        diag_blocks = get_diag()
        assert len(diag_blocks) % per == 0
        lane1 = lax.broadcasted_iota(jnp.int32, (INV_BASE, n), 1)
        lane2 = lax.broadcasted_iota(jnp.int32, (INV_BASE, per * n), 1)
        sub2 = lax.broadcasted_iota(jnp.int32, (INV_BASE, per * n), 0)
        packed = [diag_blocks[p] if per == 1 else
                  jnp.concatenate([diag_blocks[per * p + q] for q in range(per)], axis=1)
                  for p in range(len(diag_blocks) // per)]
        eye = ((lane2 % INV_BASE) == sub2).astype(F32)
        xs = [eye for _ in packed]
        grp = (lane2 // INV_BASE) * INV_BASE
        for j in range(INV_BASE - 1):
            cols = [jnp.take_along_axis(nc, grp + j, axis=1) for nc in packed]
            xs = [x - c * x[j:j + 1, :] for x, c in zip(xs, cols)]

        def expand(xc):
            xc = xc.astype(BF16)
            zero_rows = jnp.zeros_like(xc)
            return jnp.concatenate([jnp.where((lane1 // INV_BASE) == rb, xc, zero_rows) for rb in range(nblk)],
                                   axis=0)

        out["inv"] = [expand(x[:, q * n:(q + 1) * n]) for x in xs for q in range(per)]

    def merge(bs):
        def run():
            row = lax.broadcasted_iota(jnp.int32, (n, n), 0)
            col = lax.broadcasted_iota(jnp.int32, (n, n), 1)
            zero = jnp.zeros((n, n), BF16)
            cmask = (((row // (2 * bs)) == (col // (2 * bs))) & (((row // bs) % 2) == 1)
                     & (((col // bs) % 2) == 0))
            ds = out["inv"]
            cs = [jnp.where(cmask, a, zero) for a in get_mats()]
            t1 = [_dot(d, c) for d, c in zip(ds, cs)]
            t2 = [_dot(t.astype(BF16), d) for t, d in zip(t1, ds)]
            out["inv"] = [jnp.where(cmask, (-t).astype(BF16), d) for d, t in zip(ds, t2)]
        return run

    levels = []
    bs = INV_BASE
    while bs < n:
        levels.append(merge(bs))
        bs *= 2
    return [eliminate] + levels


def _dn_kernel(q_ref, k_ref, v_ref, z_ref, ba_ref, cwq_ref, cwk_ref, cwv_ref,
               ng_ref, sf0_ref, sb0_ref,
               ya_ref, sf_ref, sb_ref,
               padq_ref, padk_ref, padv_ref, qs_ref, ks_ref, vs_ref,
               r_ref, mq_ref, et_ref, o_ref, *prep_refs, seq, hps):
    t = seq
    c_len = DN_CHUNK
    n = t // c_len
    step = pl.program_id(0)
    head0 = (jnp.maximum(step - 1, 0) % (H_A // hps)) * hps
    slot_new = step % 2
    slot_cur = 1 - slot_new
    rb = min(t, 256)
    heads = [slice(g * DK, (g + 1) * DK) for g in range(hps)]

    zeros8 = jnp.zeros((8, DK), F32)
    for src_ref, pad_ref in ((q_ref, padq_ref), (k_ref, padk_ref), (v_ref, padv_ref)):
        for g, l in enumerate(heads):
            pad_ref[g, 0:8, :] = zeros8
            pad_ref[g, 8 + t:16 + t, :] = zeros8
            for r in range(0, t, rb):
                pad_ref[g, 8 + r:8 + r + rb, :] = src_ref[r:r + rb, l].astype(F32)

    def conv_block(i):
        r0 = pl.multiple_of(i * c_len, c_len)
        for pad_ref, cw_ref, dst_ref, l2, scale in ((padq_ref, cwq_ref, qs_ref, True, DK ** -0.5),
                                                    (padk_ref, cwk_ref, ks_ref, True, 1.0),
                                                    (padv_ref, cwv_ref, vs_ref, False, 1.0)):
            for g, l in enumerate(heads):
                acc = cw_ref[0:1, l] * pad_ref[g, pl.ds(r0 + 6, c_len), :]
                for j in range(1, CONV_K):
                    acc = acc + cw_ref[j:j + 1, l] * pad_ref[g, pl.ds(r0 + 6 + j, c_len), :]
                y = _silu(acc)
                if l2:
                    y = y * (lax.rsqrt(jnp.sum(y * y, axis=-1, keepdims=True) + EPS) * scale)
                dst_ref[slot_new, g, pl.ds(r0, c_len), :] = y

    @pl.when(step == 0)
    def _():
        def body(i, carry):
            conv_block(i)
            return carry
        lax.fori_loop(0, n, body, 0)

    @pl.when(step > 0)
    def _():
        _dn_group(z_ref, ba_ref, ng_ref, sf0_ref, sb0_ref, ya_ref, sf_ref, sb_ref,
                  qs_ref.at[slot_cur], ks_ref.at[slot_cur], vs_ref.at[slot_cur],
                  r_ref, mq_ref, et_ref, o_ref, prep_refs, conv_block, head0, t, hps)


def _dn_group(z_ref, ba_ref, ng_ref, sf0_ref, sb0_ref, ya_ref, sf_ref, sb_ref, qs_ref, ks_ref, vs_ref,
              r_ref, mq_ref, et_ref, o_ref, prep_refs, conv_block, head0, t, hps):
    c_len = DN_CHUNK
    n = t // c_len
    rb = min(t, 256)

    row = lax.broadcasted_iota(jnp.int32, (c_len, c_len), 0)
    col = lax.broadcasted_iota(jnp.int32, (c_len, c_len), 1)
    lower_incl = row >= col
    upper_incl = row <= col
    lower_strict = row > col
    gate_lane = [jnp.full((c_len, GATE_W), head0 + g, jnp.int32) for g in range(hps)]
    nb = max(1, min(P1_BATCH // hps, n))
    assert n % nb == 0
    n_batch = n // nb
    chunks = [(g, i) for g in range(hps) for i in range(nb)]
    items = [(g, i, d) for g, i in chunks for d in range(2)]
    pa16_ref, pdiag_ref, prhs_ref, pqkd_ref, pkdt_ref, pqe_ref, pb_ref = prep_refs

    def rows(it, stride):
        return [pl.multiple_of((it * nb + i) * stride, stride) for i in range(nb)]

    def prep_tasks(it, slot):
        r64, r8 = rows(it, c_len), rows(it, 8)
        sh = {}

        def head():
            sh["q"] = {gi: qs_ref[gi[0], pl.ds(r64[gi[1]], c_len), :] for gi in chunks}
            sh["k"] = {gi: ks_ref[gi[0], pl.ds(r64[gi[1]], c_len), :] for gi in chunks}
            kb = {gi: sh["k"][gi].astype(BF16) for gi in chunks}
            sh["kk"] = {gi: _dot_nt(kb[gi], kb[gi]) for gi in chunks}
            sh["qk"] = {gi: _dot_nt(sh["q"][gi].astype(BF16), kb[gi]) for gi in chunks}

        def item(idx, g, i, d):
            def run():
                q, k = sh["q"][g, i], sh["k"][g, i]
                gates = ba_ref[pl.ds(r64[i], c_len), :]
                b = jnp.take_along_axis(gates, gate_lane[g] + d * H_A, axis=1)
                cum = jnp.take_along_axis(gates, gate_lane[g] + (2 + d) * H_A, axis=1)
                dm = cum[:, 0:c_len] - cum[:, 0:c_len].T
                dec_sys = jnp.where(lower_incl, jnp.exp(dm if d == 0 else -dm), 0.0)
                dec_qk = dec_sys if d == 0 else jnp.where(upper_incl, jnp.exp(dm), 0.0)
                a = jnp.where(lower_strict, b[:, 0:c_len] * sh["kk"][g, i] * dec_sys, 0.0)
                e_col = jnp.exp(cum)
                v = vs_ref[g, pl.ds(r64[i], c_len), :]
                if d == 0:
                    rhs = jnp.concatenate([v * b, k * (b * e_col)], axis=1)
                else:
                    rhs = jnp.concatenate([v, k * e_col], axis=1)
                    pb_ref[slot, idx // 2] = b
                tot = cum[c_len - 1:c_len, :] if d == 0 else cum[0:1, :]
                et_ref[g, d, pl.ds(r8[i], 8), :] = jnp.exp(jnp.broadcast_to(tot, (8, DK)))
                pa16_ref[slot, idx] = a.astype(BF16)
                pdiag_ref[slot, idx] = _compress_diag_blocks(a)
                prhs_ref[slot, idx] = rhs.astype(BF16)
                pqkd_ref[slot, idx] = (sh["qk"][g, i] * dec_qk).astype(BF16)
                pkdt_ref[slot, idx] = (k * jnp.exp(tot - cum)).T.astype(BF16)
                pqe_ref[slot, idx] = q * e_col
            return run

        return [head] + [item(idx, g, i, d) for idx, (g, i, d) in enumerate(items)]

    def solve_tasks(it, slot):
        r64, r128, r192 = rows(it, c_len), rows(it, DK), rows(it, DK + c_len)
        st = {}
        inverse = _tri_inv_stages(lambda: [pdiag_ref[slot, x] for x in range(len(items))],
                                  lambda: [pa16_ref[slot, x] for x in range(len(items))], st)

        def uw_stage():
            uw = []
            for idx, (g, i, d) in enumerate(items):
                if d == 0:
                    uw.append(_dot(st["inv"][idx], prhs_ref[slot, idx]))
                else:
                    b = pb_ref[slot, idx // 2]
                    uw.append(jnp.concatenate([b, b], axis=1) * _dot_tn(st["inv"][idx], prhs_ref[slot, idx]))
            st["uwb"] = [x.astype(BF16) for x in uw]

        def out_stage():
            kuw = [_dot(pkdt_ref[slot, x], u) for x, u in enumerate(st["uwb"])]
            quw = [_dot(pqkd_ref[slot, x], u) for x, u in enumerate(st["uwb"])]
            for idx, (g, i, d) in enumerate(items):
                r_ref[g, d, pl.ds(r128[i], DK), :] = kuw[idx][:, 0:DV]
                mq_ref[g, d, pl.ds(r192[i], DK), :] = (-kuw[idx][:, DV:DV + DK]).astype(BF16)
                mq_ref[g, d, pl.ds(r192[i] + DK, c_len), :] = (pqe_ref[slot, idx]
                                                               - quw[idx][:, DV:DV + DK]).astype(BF16)
                o_ref[g, d, pl.ds(r64[i], c_len), :] = quw[idx][:, 0:DV]

        return inverse + [uw_stage, out_stage]

    def emit(solve, prep):
        per = -(-len(prep) // max(len(solve), 1)) if solve else len(prep)
        p = 0
        for s in solve:
            for task in prep[p:p + per]:
                task()
            p += per
            s()
        for task in prep[p:]:
            task()

    emit([], prep_tasks(0, 0))

    def phase1(it, carry):
        slot = it % 2
        emit(solve_tasks(it, slot), prep_tasks(it + 1, 1 - slot))
        return carry

    lax.fori_loop(0, n_batch - 1, phase1, 0)
    emit(solve_tasks(n_batch - 1, (n_batch - 1) % 2), [])


    def phase2(i, carry):
        new = []
        for g in range(hps):
            for d in range(2):
                s = carry[2 * g + d]
                c = i if d == 0 else n - 1 - i
                r64 = pl.multiple_of(c * c_len, c_len)
                r128 = pl.multiple_of(c * DK, DK)
                r192 = pl.multiple_of(c * (DK + c_len), DK + c_len)
                ms = _dot(mq_ref[g, d, pl.ds(r192, DK + c_len), :], s.astype(BF16))
                et = et_ref[g, d, pl.ds(pl.multiple_of(c * 8, 8), 8), :]
                new.append(s * et[0:1, :] + r_ref[g, d, pl.ds(r128, DK), :] + ms[0:DK])
                o_ref[g, d, pl.ds(r64, c_len), :] = o_ref[g, d, pl.ds(r64, c_len), :] + ms[DK:DK + c_len]
        conv_block(i)
        return tuple(new)

    init = []
    for g in range(hps):
        init += [sf0_ref[g], sb0_ref[g]]
    fin = lax.fori_loop(0, n, phase2, tuple(init))
    for g in range(hps):
        sf_ref[g] = fin[2 * g]
        sb_ref[g] = fin[2 * g + 1]

    for g in range(hps):
        l = slice(g * DV, (g + 1) * DV)
        for r in range(0, t, rb):
            o = o_ref[g, 0, r:r + rb, :] + o_ref[g, 1, r:r + rb, :]
            ms = jnp.mean(o * o, axis=-1, keepdims=True)
            y = o * lax.rsqrt(ms + EPS) * ng_ref[...]
            ya_ref[r:r + rb, l] = (y * _silu(z_ref[r:r + rb, l].astype(F32))).astype(BF16)


def _deltanet(qkv, z, ba, conv_w, norm_g, s_f0, s_b0, *, hps):
    bsz, t, _ = qkv.shape
    n = t // DN_CHUNK
    hb = H_A // hps
    n_grp = bsz * hb
    prep = lambda s: jnp.minimum(s, n_grp - 1)
    run = lambda s: jnp.maximum(s - 1, 0)
    prep_col = lambda off: (lambda s: (prep(s) // hb, 0, off + prep(s) % hb))
    prep_cw = lambda off: (lambda s: (0, off + prep(s) % hb))
    run_col = lambda s: (run(s) // hb, 0, run(s) % hb)
    run_state = lambda s: (run(s) // hb, run(s) % hb, 0, 0)
    w = hps * DK
    n_items = 2 * hps * max(1, min(P1_BATCH // hps, n))
    return pl.pallas_call(
        functools.partial(_dn_kernel, seq=t, hps=hps),
        grid=(n_grp + 1,),
        in_specs=[
            pl.BlockSpec((None, t, w), prep_col(0)),
            pl.BlockSpec((None, t, w), prep_col(hb)),
            pl.BlockSpec((None, t, w), prep_col(2 * hb)),
            pl.BlockSpec((None, t, w), run_col),
            pl.BlockSpec((None, t, GATE_W), lambda s: (run(s) // hb, 0, 0)),
            pl.BlockSpec((CONV_K, w), prep_cw(0)),
            pl.BlockSpec((CONV_K, w), prep_cw(hb)),
            pl.BlockSpec((CONV_K, w), prep_cw(2 * hb)),
            pl.BlockSpec((1, DV), lambda s: (0, 0)),
            pl.BlockSpec((None, hps, DK, DV), run_state),
            pl.BlockSpec((None, hps, DK, DV), run_state),
        ],
        out_specs=[
            pl.BlockSpec((None, t, w), run_col),
            pl.BlockSpec((None, hps, DK, DV), run_state),
            pl.BlockSpec((None, hps, DK, DV), run_state),
        ],
        out_shape=[
            jax.ShapeDtypeStruct((bsz, t, W_A), BF16),
            jax.ShapeDtypeStruct((bsz, H_A, DK, DV), F32),
            jax.ShapeDtypeStruct((bsz, H_A, DK, DV), F32),
        ],
        scratch_shapes=[
            pltpu.VMEM((hps, t + 16, DK), F32),
            pltpu.VMEM((hps, t + 16, DK), F32),
            pltpu.VMEM((hps, t + 16, DV), F32),
            pltpu.VMEM((2, hps, t, DK), F32),
            pltpu.VMEM((2, hps, t, DK), F32),
            pltpu.VMEM((2, hps, t, DV), F32),
            pltpu.VMEM((hps, 2, n * DK, DV), F32),
            pltpu.VMEM((hps, 2, n * (DK + DN_CHUNK), DK), BF16),
            pltpu.VMEM((hps, 2, n * 8, DV), F32),
            pltpu.VMEM((hps, 2, t, DV), F32),
            pltpu.VMEM((2, n_items, DN_CHUNK, DN_CHUNK), BF16),
            pltpu.VMEM((2, n_items, INV_BASE, DN_CHUNK), F32),
            pltpu.VMEM((2, n_items, DN_CHUNK, DV + DK), BF16),
            pltpu.VMEM((2, n_items, DN_CHUNK, DN_CHUNK), BF16),
            pltpu.VMEM((2, n_items, DK, DN_CHUNK), BF16),
            pltpu.VMEM((2, n_items, DN_CHUNK, DK), F32),
            pltpu.VMEM((2, n_items // 2, DN_CHUNK, DK), F32),
        ],
        compiler_params=pltpu.CompilerParams(vmem_limit_bytes=VMEM_LIMIT),
        name="deltanet",
    )(qkv, qkv, qkv, z, ba, conv_w, conv_w, conv_w, norm_g, s_f0, s_b0)


def _outffn_kernel(x_ref, ya_ref, yb_ref, mod_ref, n2_ref, fg_ref, wo_ref, wg_ref, wu_ref, wd_ref, o_ref,
                   act_ref, *, tm):
    subs = [slice(r, r + FFN_SUB) for r in range(0, tm, FFN_SUB)]
    hb = []
    for r in subs:
        y = _dot(ya_ref[r, :], wo_ref[0:W_A, :]) + _dot(yb_ref[r, :], wo_ref[W_A:D_MODEL, :])
        x1 = x_ref[r, :] + mod_ref[2:3, :] * y
        o_ref[r, :] = x1
        ms = jnp.mean(x1 * x1, axis=-1, keepdims=True)
        h = (x1 * lax.rsqrt(ms + EPS) * n2_ref[...]) * (1.0 + mod_ref[4:5, :]) + mod_ref[3:4, :]
        hb.append(h.astype(BF16))
    for c in range(D_FF // FF_CHUNK):
        l = slice(c * FF_CHUNK, (c + 1) * FF_CHUNK)
        for r, h in zip(subs, hb):
            act_ref[r, l] = (_silu(_dot(h, wg_ref[:, l])) * _dot(h, wu_ref[:, l])).astype(BF16)
    for r in subs:
        x2 = o_ref[r, :] + mod_ref[5:6, :] * _dot(act_ref[r, :], wd_ref[...])
        ms2 = jnp.mean(x2 * x2, axis=-1, keepdims=True)
        o_ref[r, :] = x2 * lax.rsqrt(ms2 + EPS) * fg_ref[...]


def _outffn(x2d, ya, yb, mod, n2, fg, w_out, w_gate, w_up, w_down, *, seq, row0, row_stride, tm):
    n_tok = x2d.shape[0]
    assert row_stride == 0 or seq % tm == 0

    def mod_idx(i):
        return (row0 + ((i * tm) // seq) * row_stride, 0, 0)

    const2 = lambda i: (0, 0)
    tok = lambda i: (i, 0)
    resident = lambda w: pl.BlockSpec(w.shape, const2, pipeline_mode=pl.Buffered(1))
    return pl.pallas_call(
        functools.partial(_outffn_kernel, tm=tm),
        grid=(n_tok // tm,),
        in_specs=[
            pl.BlockSpec((tm, D_MODEL), tok),
            pl.BlockSpec((tm, W_A), tok),
            pl.BlockSpec((tm, W_B), tok),
            pl.BlockSpec((None, N_MOD, D_MODEL), mod_idx),
            pl.BlockSpec((1, D_MODEL), const2),
            pl.BlockSpec((1, D_MODEL), const2),
            resident(w_out), resident(w_gate), resident(w_up), resident(w_down),
        ],
        out_specs=pl.BlockSpec((tm, D_MODEL), tok),
        out_shape=jax.ShapeDtypeStruct((n_tok, D_MODEL), F32),
        scratch_shapes=[pltpu.VMEM((tm, D_FF), BF16)],
        compiler_params=pltpu.CompilerParams(vmem_limit_bytes=VMEM_LIMIT),
        name="outffn",
    )(x2d, ya, yb, mod, n2, fg, w_out, w_gate, w_up, w_down)


def _trunk_layer(x, mod, row0, row_stride, s_f0, s_b0, p, final_g):
    bsz, t, _ = x.shape
    x2d = x.reshape(bsz * t, D_MODEL)
    qkv, z, ba, yb = _inproj(x2d, mod, p["g1"], p["w_main"], p["w_ba"], p["a_row"], p["dt_row"], p["ln_g"], p["ln_b"], p["w_s"],
                             p["b_s_t"], seq=t, row0=row0, row_stride=row_stride, tm=IN_TM)
    ya, s_f, s_b = _deltanet(qkv.reshape(bsz, t, 3 * W_A), z.reshape(bsz, t, W_A), ba.reshape(bsz, t, GATE_W),
                             p["conv_w"], p["dn_g"], s_f0, s_b0, hps=H_A if t <= DN_SHORT else 1)
    out = _outffn(x2d, ya.reshape(bsz * t, W_A), yb, mod, p["g2"], final_g, p["w_out"], p["w_gate"], p["w_up"],
                  p["w_down"], seq=t, row0=row0, row_stride=row_stride, tm=FFN_TM)
    return out.reshape(bsz, t, D_MODEL), s_f, s_b


def kernel(x_prompt, x_sample, state_fwd, state_bwd, c, c_ctx, w_ada, b_ada, norm1_g, norm2_g, w_in, conv_w,
           a_log, dt_bias, dn_norm_g, sgu_ln_g, sgu_ln_b, sgu_w, sgu_b, w_out, w_gate, w_up, w_down, final_g):
    depth = w_ada.shape[0]
    assert depth == 1, "the final RMSNorm is fused into the last layer; only depth 1 is supported"
    n_lat = c.shape[0]
    assert 1 + n_lat <= MOD_ROWS
    cc = jnp.zeros((MOD_ROWS, D_MODEL), F32).at[0].set(c_ctx).at[1:1 + n_lat].set(c)
    s_zero = jnp.zeros((x_prompt.shape[0], H_A, DK, DV), F32)
    xp, xs = x_prompt, x_sample
    new_f, new_b = [], []
    gate_pad = jnp.zeros((2 * H_A,), F32)
    for l in range(depth):
        w_in_b = w_in[l].astype(BF16)
        n_main = 4 * W_A + 2 * W_B
        p = dict(
            g1=norm1_g[l].reshape(1, D_MODEL), g2=norm2_g[l].reshape(1, D_MODEL),
            w_main=w_in_b[:, :n_main],
            w_ba=jnp.pad(w_in_b[:, n_main:], ((0, 0), (0, GATE_W - 4 * H_A))),
            ln_g=sgu_ln_g[l].reshape(1, W_B), ln_b=sgu_ln_b[l].reshape(1, W_B),
            w_s=sgu_w[l].astype(BF16), b_s_t=jnp.transpose(sgu_b[l]),
            conv_w=conv_w[l],
            a_row=jnp.pad(jnp.concatenate([gate_pad, a_log[l].reshape(-1)]), (0, GATE_W - 4 * H_A)).reshape(1, GATE_W),
            dt_row=jnp.pad(jnp.concatenate([gate_pad, dt_bias[l].reshape(-1)]), (0, GATE_W - 4 * H_A)).reshape(1, GATE_W),
            dn_g=dn_norm_g[l].reshape(1, DV),
            w_out=w_out[l].astype(BF16), w_gate=w_gate[l].astype(BF16), w_up=w_up[l].astype(BF16),
            w_down=w_down[l].astype(BF16),
        )
        mod = _ada(cc, w_ada[l], b_ada[l]).reshape(MOD_ROWS, N_MOD, D_MODEL)
        fg = final_g.reshape(1, D_MODEL)
        xp, sf, sb = _trunk_layer(xp, mod, 0, 0, s_zero, s_zero, p, fg)
        new_f.append(sf)
        new_b.append(sb)
        xs, _, _ = _trunk_layer(xs, mod, 1, 1, state_fwd[:, l], state_bwd[:, l], p, fg)
    return (xp, xs, jnp.stack(new_f, axis=1), jnp.stack(new_b, axis=1))
```

```python
import functools

import jax
import jax.numpy as jnp
from jax import lax
from jax.experimental import pallas as pl
from jax.experimental.pallas import tpu as pltpu

F32 = jnp.float32
BF16 = jnp.bfloat16

D_MODEL = 1024
W_A = D_MODEL // 2
DK = 128
DV = 128
H_A = W_A // DV
W_B = D_MODEL - W_A
WG = 128
G_B = W_B // WG
SGU_CHUNK = 128
CONV_K = 5
DN_CHUNK = 128
D_FF = 2816
N_MOD = 6
EPS = 1e-6

MOD_ROWS = 16
GATE_W = 128
LANES = 128
INV_BASE = 16
FF_CHUNK = 256
IN_SUB = 256
IN_TM = 1024
FFN_SUB = 256
FFN_TM = 1024
MIN_STEPS = 8
DN_SHORT = 512
P1_BATCH = 8
VMEM_LIMIT = 56 * 1024 * 1024
VMEM_LIMIT_DN = 58 * 1024 * 1024


def _sigmoid(x):
    return 0.5 + 0.5 * jnp.tanh(0.5 * x)


def _silu(x):
    hx = 0.5 * x
    return hx + hx * jnp.tanh(hx)


def _gelu_tanh(x):
    c = 0.7978845608028654
    return 0.5 * x * (1.0 + jnp.tanh(c * (x + 0.044715 * (x * x * x))))


def _softplus(x):
    return jnp.maximum(x, 0.0) + jnp.log(1.0 + jnp.exp(-jnp.abs(x)))


def _dot(a, b):
    return jnp.dot(a, b, preferred_element_type=F32)


def _dot_nt(a, b):
    return lax.dot_general(a, b, (((1,), (1,)), ((), ())), preferred_element_type=F32)


def _dot_tn(a, b):
    return lax.dot_general(a, b, (((0,), (0,)), ((), ())), preferred_element_type=F32)


def _split3(x):
    hi = x.astype(BF16)
    r1 = x - hi.astype(F32)
    mid = r1.astype(BF16)
    lo = (r1 - mid.astype(F32)).astype(BF16)
    return hi, mid, lo


def _dot_exact_lhs(a_bf16, x):
    hi, mid, lo = _split3(x)
    return _dot(a_bf16, hi) + _dot(a_bf16, mid) + _dot(a_bf16, lo)


def _ada_kernel(c_ref, w_ref, b_ref, o_ref):
    s = _silu(c_ref[...])
    o_ref[...] = _dot(s.astype(BF16), w_ref[...].astype(BF16)) + b_ref[...]


def _ada(cc, w_ada, b_ada):
    n_out = w_ada.shape[1]
    bn = D_MODEL
    return pl.pallas_call(
        _ada_kernel,
        grid=(n_out // bn,),
        in_specs=[
            pl.BlockSpec((MOD_ROWS, D_MODEL), lambda j: (0, 0)),
            pl.BlockSpec((D_MODEL, bn), lambda j: (0, j)),
            pl.BlockSpec((1, bn), lambda j: (0, j)),
        ],
        out_specs=pl.BlockSpec((MOD_ROWS, bn), lambda j: (0, j)),
        out_shape=jax.ShapeDtypeStruct((MOD_ROWS, n_out), F32),
        compiler_params=pltpu.CompilerParams(vmem_limit_bytes=VMEM_LIMIT),
        name="ada",
    )(cc, w_ada, b_ada.reshape(1, n_out))


def _inproj_kernel(x_ref, mod_ref, g1_ref, wm_ref, wba_ref, arow_ref, dtrow_ref, lng_ref, lnb_ref, ws_ref, bs_ref,
                   qkv_ref, z_ref, ba_ref, yb_ref, *, tm):
    subs = [slice(r, r + IN_SUB) for r in range(0, tm, IN_SUB)]
    ri = lax.broadcasted_iota(jnp.int32, (IN_SUB, IN_SUB), 0)
    ci = lax.broadcasted_iota(jnp.int32, (IN_SUB, IN_SUB), 1)
    same_chunk = (ri // DN_CHUNK) == (ci // DN_CHUNK)
    prefix_op = (same_chunk & (ri >= ci)).astype(BF16)

    def norm(r):
        x = x_ref[r, :]
        ms = jnp.mean(x * x, axis=-1, keepdims=True)
        xn = x * lax.rsqrt(ms + EPS) * g1_ref[...]
        return (xn * (1.0 + mod_ref[1:2, :]) + mod_ref[0:1, :]).astype(BF16)

    def matmul_tasks(r, hb):
        def qkv_piece(j):
            def run():
                l = slice(j * W_A, (j + 1) * W_A)
                qkv_ref[r, l] = _dot(hb, wm_ref[:, l]).astype(BF16)
            return run

        def z_piece():
            z_ref[r, :] = _dot(hb, wm_ref[:, 3 * W_A:4 * W_A]).astype(BF16)

        def gate_piece():
            logit = _dot(hb, wba_ref[...])
            lane = lax.broadcasted_iota(jnp.int32, logit.shape, 1)
            g_all = -jnp.exp(arow_ref[...]) * _softplus(logit + dtrow_ref[...])
            cum_f = _dot_exact_lhs(prefix_op, g_all)
            tot = jnp.concatenate(
                [jnp.broadcast_to(cum_f[c + DN_CHUNK - 1:c + DN_CHUNK, :], (DN_CHUNK, GATE_W))
                 for c in range(0, IN_SUB, DN_CHUNK)], axis=0)
            cum_b = tot - cum_f + g_all
            ba_ref[r, :] = jnp.where(lane < 2 * H_A, _sigmoid(logit), jnp.where(lane < 3 * H_A, cum_f, cum_b))

        return [qkv_piece(0), qkv_piece(1), qkv_piece(2), z_piece, gate_piece]

    def sgu_tasks(r, st):
        def gelu_u():
            st["u"] = _gelu_tanh(st["u"])

        def gelu_v():
            st["v"] = _gelu_tanh(st["v"])

        def layernorm_v():
            vb = st["v"]
            mu = jnp.mean(vb, axis=-1, keepdims=True)
            vc = vb - mu
            var = jnp.mean(vc * vc, axis=-1, keepdims=True)
            st["v"] = (vc * lax.rsqrt(var + EPS) * lng_ref[...] + lnb_ref[...]).astype(BF16)

        def mix(c):
            def run():
                rr = slice(c * SGU_CHUNK, (c + 1) * SGU_CHUNK)
                ro = slice(r.start + c * SGU_CHUNK, r.start + (c + 1) * SGU_CHUNK)
                for g in range(G_B):
                    l = slice(g * WG, (g + 1) * WG)
                    mixed = _dot(ws_ref[g], st["v"][rr, l]) + bs_ref[:, g:g + 1]
                    yb_ref[ro, l] = (st["u"][rr, l] * mixed).astype(BF16)
            return run

        return [gelu_u, gelu_v, layernorm_v] + [mix(c) for c in range(IN_SUB // SGU_CHUNK)]

    hb = norm(subs[0])
    for i, r in enumerate(subs):
        st = {"u": _dot(hb, wm_ref[:, 4 * W_A:4 * W_A + W_B]),
              "v": _dot(hb, wm_ref[:, 4 * W_A + W_B:4 * W_A + 2 * W_B])}
        mm = matmul_tasks(r, hb)
        ew = sgu_tasks(r, st)
        nxt = {}
        if i + 1 < len(subs):
            ew.append(lambda i=i: nxt.update(hb=norm(subs[i + 1])))
        for k in range(max(len(mm), len(ew))):
            if k < len(mm):
                mm[k]()
            if k < len(ew):
                ew[k]()
        hb = nxt.get("hb")


def _inproj(x2d, mod, g1, w_main, w_ba, a_row, dt_row, ln_g, ln_b, w_s, b_s_t, *, seq, row0, row_stride, tm):
    n_tok = x2d.shape[0]
    assert row_stride == 0 or seq % tm == 0

    def mod_idx(i):
        return (row0 + ((i * tm) // seq) * row_stride, 0, 0)

    const2 = lambda i: (0, 0)
    tok = lambda i: (i, 0)
    return pl.pallas_call(
        functools.partial(_inproj_kernel, tm=tm),
        grid=(n_tok // tm,),
        in_specs=[
            pl.BlockSpec((tm, D_MODEL), tok),
            pl.BlockSpec((None, N_MOD, D_MODEL), mod_idx),
            pl.BlockSpec((1, D_MODEL), const2),
            pl.BlockSpec(w_main.shape, const2, pipeline_mode=pl.Buffered(1)),
            pl.BlockSpec(w_ba.shape, const2),
            pl.BlockSpec((1, GATE_W), const2),
            pl.BlockSpec((1, GATE_W), const2),
            pl.BlockSpec((1, W_B), const2),
            pl.BlockSpec((1, W_B), const2),
            pl.BlockSpec(w_s.shape, lambda i: (0, 0, 0)),
            pl.BlockSpec(b_s_t.shape, const2),
        ],
        out_specs=[
            pl.BlockSpec((tm, 3 * W_A), tok),
            pl.BlockSpec((tm, W_A), tok),
            pl.BlockSpec((tm, GATE_W), tok),
            pl.BlockSpec((tm, W_B), tok),
        ],
        out_shape=[
            jax.ShapeDtypeStruct((n_tok, 3 * W_A), BF16),
            jax.ShapeDtypeStruct((n_tok, W_A), BF16),
            jax.ShapeDtypeStruct((n_tok, GATE_W), F32),
            jax.ShapeDtypeStruct((n_tok, W_B), BF16),
        ],
        compiler_params=pltpu.CompilerParams(vmem_limit_bytes=VMEM_LIMIT),
        name="inproj",
    )(x2d, mod, g1, w_main, w_ba, a_row, dt_row, ln_g, ln_b, w_s, b_s_t)


def _compress_diag_blocks(a):
    n = a.shape[0]
    lane = lax.broadcasted_iota(jnp.int32, (INV_BASE, n), 1)
    out = jnp.where((lane // INV_BASE) == 0, a[0:INV_BASE, :], 0.0)
    for rb in range(1, n // INV_BASE):
        out = out + jnp.where((lane // INV_BASE) == rb, a[rb * INV_BASE:(rb + 1) * INV_BASE, :], 0.0)
    return out


def _tri_inv_stages(get_diag, get_mats, out):
    n = DN_CHUNK
    nblk = n // INV_BASE
    per = LANES // n
    assert per >= 1

    def eliminate():
        diag_blocks = get_diag()
        assert len(diag_blocks) % per == 0
        lane1 = lax.broadcasted_iota(jnp.int32, (INV_BASE, n), 1)
        lane2 = lax.broadcasted_iota(jnp.int32, (INV_BASE, per * n), 1)
        sub2 = lax.broadcasted_iota(jnp.int32, (INV_BASE, per * n), 0)
        packed = [diag_blocks[p] if per == 1 else
                  jnp.concatenate([diag_blocks[per * p + q] for q in range(per)], axis=1)
                  for p in range(len(diag_blocks) // per)]
        eye = ((lane2 % INV_BASE) == sub2).astype(F32)
        xs = [eye for _ in packed]
        grp = (lane2 // INV_BASE) * INV_BASE
        for j in range(INV_BASE - 1):
            cols = [jnp.take_along_axis(nc, grp + j, axis=1) for nc in packed]
            xs = [x - c * x[j:j + 1, :] for x, c in zip(xs, cols)]

        def expand(xc):
            xc = xc.astype(BF16)
            zero_rows = jnp.zeros_like(xc)
            return jnp.concatenate([jnp.where((lane1 // INV_BASE) == rb, xc, zero_rows) for rb in range(nblk)],
                                   axis=0)

        out["inv"] = [expand(x[:, q * n:(q + 1) * n]) for x in xs for q in range(per)]

    def merge(bs):
        def run():
            row = lax.broadcasted_iota(jnp.int32, (n, n), 0)
            col = lax.broadcasted_iota(jnp.int32, (n, n), 1)
            zero = jnp.zeros((n, n), BF16)
            cmask = (((row // (2 * bs)) == (col // (2 * bs))) & (((row // bs) % 2) == 1)
                     & (((col // bs) % 2) == 0))
            ds = out["inv"]
            cs = [jnp.where(cmask, a, zero) for a in get_mats()]
            t1 = [_dot(d, c) for d, c in zip(ds, cs)]
            t2 = [_dot(t.astype(BF16), d) for t, d in zip(t1, ds)]
            out["inv"] = [jnp.where(cmask, (-t).astype(BF16), d) for d, t in zip(ds, t2)]
        return run

    levels = []
    bs = INV_BASE
    while bs < n:
        levels.append(merge(bs))
        bs *= 2
    return [eliminate] + levels


def _dn_kernel(q_ref, k_ref, v_ref, z_ref, ba_ref, cwq_ref, cwk_ref, cwv_ref,
               ng_ref, sf0_ref, sb0_ref,
               ya_ref, sf_ref, sb_ref,
               padq_ref, padk_ref, padv_ref, qs_ref, ks_ref, vs_ref,
               r_ref, mq_ref, et_ref, o_ref, *prep_refs, seq, hps):
    t = seq
    c_len = DN_CHUNK
    n = t // c_len
    step = pl.program_id(0)
    head0 = (jnp.maximum(step - 1, 0) % (H_A // hps)) * hps
    slot_new = step % 2
    slot_cur = 1 - slot_new
    rb = min(t, 256)
    heads = [slice(g * DK, (g + 1) * DK) for g in range(hps)]

    zeros8 = jnp.zeros((8, DK), F32)
    for src_ref, pad_ref in ((q_ref, padq_ref), (k_ref, padk_ref), (v_ref, padv_ref)):
        for g, l in enumerate(heads):
            pad_ref[g, 0:8, :] = zeros8
            pad_ref[g, 8 + t:16 + t, :] = zeros8
            for r in range(0, t, rb):
                pad_ref[g, 8 + r:8 + r + rb, :] = src_ref[r:r + rb, l].astype(F32)

    def conv_block(i):
        r0 = pl.multiple_of(i * c_len, c_len)
        for pad_ref, cw_ref, dst_ref, l2, scale in ((padq_ref, cwq_ref, qs_ref, True, DK ** -0.5),
                                                    (padk_ref, cwk_ref, ks_ref, True, 1.0),
                                                    (padv_ref, cwv_ref, vs_ref, False, 1.0)):
            for g, l in enumerate(heads):
                acc = cw_ref[0:1, l] * pad_ref[g, pl.ds(r0 + 6, c_len), :]
                for j in range(1, CONV_K):
                    acc = acc + cw_ref[j:j + 1, l] * pad_ref[g, pl.ds(r0 + 6 + j, c_len), :]
                y = _silu(acc)
                if l2:
                    y = y * (lax.rsqrt(jnp.sum(y * y, axis=-1, keepdims=True) + EPS) * scale)
                dst_ref[slot_new, g, pl.ds(r0, c_len), :] = y

    @pl.when(step == 0)
    def _():
        def body(i, carry):
            conv_block(i)
            return carry
        lax.fori_loop(0, n, body, 0)

    @pl.when(step > 0)
    def _():
        _dn_group(z_ref, ba_ref, ng_ref, sf0_ref, sb0_ref, ya_ref, sf_ref, sb_ref,
                  qs_ref.at[slot_cur], ks_ref.at[slot_cur], vs_ref.at[slot_cur],
                  r_ref, mq_ref, et_ref, o_ref, prep_refs, conv_block, head0, t, hps)


def _dn_group(z_ref, ba_ref, ng_ref, sf0_ref, sb0_ref, ya_ref, sf_ref, sb_ref, qs_ref, ks_ref, vs_ref,
              r_ref, mq_ref, et_ref, o_ref, prep_refs, conv_block, head0, t, hps):
    c_len = DN_CHUNK
    n = t // c_len
    rb = min(t, 256)

    row = lax.broadcasted_iota(jnp.int32, (c_len, c_len), 0)
    col = lax.broadcasted_iota(jnp.int32, (c_len, c_len), 1)
    lower_incl = row >= col
    upper_incl = row <= col
    lower_strict = row > col
    gate_lane = [jnp.full((c_len, GATE_W), head0 + g, jnp.int32) for g in range(hps)]
    nb = max(1, min(P1_BATCH // hps, n))
    assert n % nb == 0
    n_batch = n // nb
    chunks = [(g, i) for g in range(hps) for i in range(nb)]
    items = [(g, i, d) for g, i in chunks for d in range(2)]
    pa16_ref, pdiag_ref, prhs_ref, pqkd_ref, pkdt_ref, pqe_ref, pb_ref = prep_refs

    def rows(it, stride):
        return [pl.multiple_of((it * nb + i) * stride, stride) for i in range(nb)]

    def prep_tasks(it, slot):
        r64, r8 = rows(it, c_len), rows(it, 8)
        sh = {}

        def head():
            sh["q"] = {gi: qs_ref[gi[0], pl.ds(r64[gi[1]], c_len), :] for gi in chunks}
            sh["k"] = {gi: ks_ref[gi[0], pl.ds(r64[gi[1]], c_len), :] for gi in chunks}
            kb = {gi: sh["k"][gi].astype(BF16) for gi in chunks}
            sh["kk"] = {gi: _dot_nt(kb[gi], kb[gi]) for gi in chunks}
            sh["qk"] = {gi: _dot_nt(sh["q"][gi].astype(BF16), kb[gi]) for gi in chunks}

        def item(idx, g, i, d):
            def run():
                q, k = sh["q"][g, i], sh["k"][g, i]
                gates = ba_ref[pl.ds(r64[i], c_len), :]
                b = jnp.take_along_axis(gates, gate_lane[g] + d * H_A, axis=1)
                cum = jnp.take_along_axis(gates, gate_lane[g] + (2 + d) * H_A, axis=1)
                dm = cum[:, 0:c_len] - cum[:, 0:c_len].T
                dec_sys = jnp.where(lower_incl, jnp.exp(dm if d == 0 else -dm), 0.0)
                dec_qk = dec_sys if d == 0 else jnp.where(upper_incl, jnp.exp(dm), 0.0)
                a = jnp.where(lower_strict, b[:, 0:c_len] * sh["kk"][g, i] * dec_sys, 0.0)
                e_col = jnp.exp(cum)
                v = vs_ref[g, pl.ds(r64[i], c_len), :]
                if d == 0:
                    rhs = jnp.concatenate([v * b, k * (b * e_col)], axis=1)
                else:
                    rhs = jnp.concatenate([v, k * e_col], axis=1)
                    pb_ref[slot, idx // 2] = b
                tot = cum[c_len - 1:c_len, :] if d == 0 else cum[0:1, :]
                et_ref[g, d, pl.ds(r8[i], 8), :] = jnp.exp(jnp.broadcast_to(tot, (8, DK)))
                pa16_ref[slot, idx] = a.astype(BF16)
                pdiag_ref[slot, idx] = _compress_diag_blocks(a)
                prhs_ref[slot, idx] = rhs.astype(BF16)
                pqkd_ref[slot, idx] = (sh["qk"][g, i] * dec_qk).astype(BF16)
                pkdt_ref[slot, idx] = (k * jnp.exp(tot - cum)).T.astype(BF16)
                pqe_ref[slot, idx] = q * e_col
            return run

        return [head] + [item(idx, g, i, d) for idx, (g, i, d) in enumerate(items)]

    def solve_tasks(it, slot):
        r64, r128, r192 = rows(it, c_len), rows(it, DK), rows(it, DK + c_len)
        st = {}
        inverse = _tri_inv_stages(lambda: [pdiag_ref[slot, x] for x in range(len(items))],
                                  lambda: [pa16_ref[slot, x] for x in range(len(items))], st)

        def uw_stage():
            uw = []
            for idx, (g, i, d) in enumerate(items):
                if d == 0:
                    uw.append(_dot(st["inv"][idx], prhs_ref[slot, idx]))
                else:
                    b = pb_ref[slot, idx // 2]
                    uw.append(jnp.concatenate([b, b], axis=1) * _dot_tn(st["inv"][idx], prhs_ref[slot, idx]))
            st["uwb"] = [x.astype(BF16) for x in uw]

        def out_stage():
            kuw = [_dot(pkdt_ref[slot, x], u) for x, u in enumerate(st["uwb"])]
            quw = [_dot(pqkd_ref[slot, x], u) for x, u in enumerate(st["uwb"])]
            for idx, (g, i, d) in enumerate(items):
                r_ref[g, d, pl.ds(r128[i], DK), :] = kuw[idx][:, 0:DV]
                mq_ref[g, d, pl.ds(r192[i], DK), :] = (-kuw[idx][:, DV:DV + DK]).astype(BF16)
                mq_ref[g, d, pl.ds(r192[i] + DK, c_len), :] = (pqe_ref[slot, idx]
                                                               - quw[idx][:, DV:DV + DK]).astype(BF16)
                o_ref[g, d, pl.ds(r64[i], c_len), :] = quw[idx][:, 0:DV]

        return inverse + [uw_stage, out_stage]

    def emit(solve, prep):
        per = -(-len(prep) // max(len(solve), 1)) if solve else len(prep)
        p = 0
        for s in solve:
            for task in prep[p:p + per]:
                task()
            p += per
            s()
        for task in prep[p:]:
            task()

    emit([], prep_tasks(0, 0))

    def phase1(it, carry):
        slot = it % 2
        emit(solve_tasks(it, slot), prep_tasks(it + 1, 1 - slot))
        return carry

    lax.fori_loop(0, n_batch - 1, phase1, 0)
    emit(solve_tasks(n_batch - 1, (n_batch - 1) % 2), [])


    def phase2(i, carry):
        new = []
        for g in range(hps):
            for d in range(2):
                s = carry[2 * g + d]
                c = i if d == 0 else n - 1 - i
                r64 = pl.multiple_of(c * c_len, c_len)
                r128 = pl.multiple_of(c * DK, DK)
                r192 = pl.multiple_of(c * (DK + c_len), DK + c_len)
                ms = _dot(mq_ref[g, d, pl.ds(r192, DK + c_len), :], s.astype(BF16))
                et = et_ref[g, d, pl.ds(pl.multiple_of(c * 8, 8), 8), :]
                new.append(s * et[0:1, :] + r_ref[g, d, pl.ds(r128, DK), :] + ms[0:DK])
                o_ref[g, d, pl.ds(r64, c_len), :] = o_ref[g, d, pl.ds(r64, c_len), :] + ms[DK:DK + c_len]
        conv_block(i)
        return tuple(new)

    init = []
    for g in range(hps):
        init += [sf0_ref[g], sb0_ref[g]]
    fin = lax.fori_loop(0, n, phase2, tuple(init))
    for g in range(hps):
        sf_ref[g] = fin[2 * g]
        sb_ref[g] = fin[2 * g + 1]

    for g in range(hps):
        l = slice(g * DV, (g + 1) * DV)
        for r in range(0, t, rb):
            o = o_ref[g, 0, r:r + rb, :] + o_ref[g, 1, r:r + rb, :]
            ms = jnp.mean(o * o, axis=-1, keepdims=True)
            y = o * lax.rsqrt(ms + EPS) * ng_ref[...]
            ya_ref[r:r + rb, l] = (y * _silu(z_ref[r:r + rb, l].astype(F32))).astype(BF16)


def _deltanet(qkv, z, ba, conv_w, norm_g, s_f0, s_b0, *, hps):
    bsz, t, _ = qkv.shape
    n = t // DN_CHUNK
    hb = H_A // hps
    n_grp = bsz * hb
    prep = lambda s: jnp.minimum(s, n_grp - 1)
    run = lambda s: jnp.maximum(s - 1, 0)
    prep_col = lambda off: (lambda s: (prep(s) // hb, 0, off + prep(s) % hb))
    prep_cw = lambda off: (lambda s: (0, off + prep(s) % hb))
    run_col = lambda s: (run(s) // hb, 0, run(s) % hb)
    run_state = lambda s: (run(s) // hb, run(s) % hb, 0, 0)
    w = hps * DK
    n_items = 2 * hps * max(1, min(P1_BATCH // hps, n))
    return pl.pallas_call(
        functools.partial(_dn_kernel, seq=t, hps=hps),
        grid=(n_grp + 1,),
        in_specs=[
            pl.BlockSpec((None, t, w), prep_col(0)),
            pl.BlockSpec((None, t, w), prep_col(hb)),
            pl.BlockSpec((None, t, w), prep_col(2 * hb)),
            pl.BlockSpec((None, t, w), run_col),
            pl.BlockSpec((None, t, GATE_W), lambda s: (run(s) // hb, 0, 0)),
            pl.BlockSpec((CONV_K, w), prep_cw(0)),
            pl.BlockSpec((CONV_K, w), prep_cw(hb)),
            pl.BlockSpec((CONV_K, w), prep_cw(2 * hb)),
            pl.BlockSpec((1, DV), lambda s: (0, 0)),
            pl.BlockSpec((None, hps, DK, DV), run_state),
            pl.BlockSpec((None, hps, DK, DV), run_state),
        ],
        out_specs=[
            pl.BlockSpec((None, t, w), run_col),
            pl.BlockSpec((None, hps, DK, DV), run_state),
            pl.BlockSpec((None, hps, DK, DV), run_state),
        ],
        out_shape=[
            jax.ShapeDtypeStruct((bsz, t, W_A), BF16),
            jax.ShapeDtypeStruct((bsz, H_A, DK, DV), F32),
            jax.ShapeDtypeStruct((bsz, H_A, DK, DV), F32),
        ],
        scratch_shapes=[
            pltpu.VMEM((hps, t + 16, DK), F32),
            pltpu.VMEM((hps, t + 16, DK), F32),
            pltpu.VMEM((hps, t + 16, DV), F32),
            pltpu.VMEM((2, hps, t, DK), F32),
            pltpu.VMEM((2, hps, t, DK), F32),
            pltpu.VMEM((2, hps, t, DV), F32),
            pltpu.VMEM((hps, 2, n * DK, DV), F32),
            pltpu.VMEM((hps, 2, n * (DK + DN_CHUNK), DK), BF16),
            pltpu.VMEM((hps, 2, n * 8, DV), F32),
            pltpu.VMEM((hps, 2, t, DV), F32),
            pltpu.VMEM((2, n_items, DN_CHUNK, DN_CHUNK), BF16),
            pltpu.VMEM((2, n_items, INV_BASE, DN_CHUNK), F32),
            pltpu.VMEM((2, n_items, DN_CHUNK, DV + DK), BF16),
            pltpu.VMEM((2, n_items, DN_CHUNK, DN_CHUNK), BF16),
            pltpu.VMEM((2, n_items, DK, DN_CHUNK), BF16),
            pltpu.VMEM((2, n_items, DN_CHUNK, DK), F32),
            pltpu.VMEM((2, n_items // 2, DN_CHUNK, DK), F32),
        ],
        compiler_params=pltpu.CompilerParams(vmem_limit_bytes=VMEM_LIMIT_DN),
        name="deltanet",
    )(qkv, qkv, qkv, z, ba, conv_w, conv_w, conv_w, norm_g, s_f0, s_b0)


def _outffn_kernel(x_ref, ya_ref, yb_ref, mod_ref, n2_ref, fg_ref, wo_ref, wg_ref, wu_ref, wd_ref, o_ref,
                   act_ref, *, tm):
    subs = [slice(r, r + FFN_SUB) for r in range(0, tm, FFN_SUB)]
    hb = []
    for r in subs:
        y = _dot(ya_ref[r, :], wo_ref[0:W_A, :]) + _dot(yb_ref[r, :], wo_ref[W_A:D_MODEL, :])
        x1 = x_ref[r, :] + mod_ref[2:3, :] * y
        o_ref[r, :] = x1
        ms = jnp.mean(x1 * x1, axis=-1, keepdims=True)
        h = (x1 * lax.rsqrt(ms + EPS) * n2_ref[...]) * (1.0 + mod_ref[4:5, :]) + mod_ref[3:4, :]
        hb.append(h.astype(BF16))
    for c in range(D_FF // FF_CHUNK):
        l = slice(c * FF_CHUNK, (c + 1) * FF_CHUNK)
        for r, h in zip(subs, hb):
            act_ref[r, l] = (_silu(_dot(h, wg_ref[:, l])) * _dot(h, wu_ref[:, l])).astype(BF16)
    for r in subs:
        x2 = o_ref[r, :] + mod_ref[5:6, :] * _dot(act_ref[r, :], wd_ref[...])
        ms2 = jnp.mean(x2 * x2, axis=-1, keepdims=True)
        o_ref[r, :] = x2 * lax.rsqrt(ms2 + EPS) * fg_ref[...]


def _outffn(x2d, ya, yb, mod, n2, fg, w_out, w_gate, w_up, w_down, *, seq, row0, row_stride, tm):
    n_tok = x2d.shape[0]
    assert row_stride == 0 or seq % tm == 0

    def mod_idx(i):
        return (row0 + ((i * tm) // seq) * row_stride, 0, 0)

    const2 = lambda i: (0, 0)
    tok = lambda i: (i, 0)
    resident = lambda w: pl.BlockSpec(w.shape, const2, pipeline_mode=pl.Buffered(1))
    return pl.pallas_call(
        functools.partial(_outffn_kernel, tm=tm),
        grid=(n_tok // tm,),
        in_specs=[
            pl.BlockSpec((tm, D_MODEL), tok),
            pl.BlockSpec((tm, W_A), tok),
            pl.BlockSpec((tm, W_B), tok),
            pl.BlockSpec((None, N_MOD, D_MODEL), mod_idx),
            pl.BlockSpec((1, D_MODEL), const2),
            pl.BlockSpec((1, D_MODEL), const2),
            resident(w_out), resident(w_gate), resident(w_up), resident(w_down),
        ],
        out_specs=pl.BlockSpec((tm, D_MODEL), tok),
        out_shape=jax.ShapeDtypeStruct((n_tok, D_MODEL), F32),
        scratch_shapes=[pltpu.VMEM((tm, D_FF), BF16)],
        compiler_params=pltpu.CompilerParams(vmem_limit_bytes=VMEM_LIMIT),
        name="outffn",
    )(x2d, ya, yb, mod, n2, fg, w_out, w_gate, w_up, w_down)


def _trunk_layer(x, mod, row0, row_stride, s_f0, s_b0, p, final_g):
    bsz, t, _ = x.shape
    x2d = x.reshape(bsz * t, D_MODEL)
    qkv, z, ba, yb = _inproj(x2d, mod, p["g1"], p["w_main"], p["w_ba"], p["a_row"], p["dt_row"], p["ln_g"], p["ln_b"], p["w_s"],
                             p["b_s_t"], seq=t, row0=row0, row_stride=row_stride, tm=IN_TM)
    ya, s_f, s_b = _deltanet(qkv.reshape(bsz, t, 3 * W_A), z.reshape(bsz, t, W_A), ba.reshape(bsz, t, GATE_W),
                             p["conv_w"], p["dn_g"], s_f0, s_b0, hps=H_A if t <= DN_SHORT else 2)
    out = _outffn(x2d, ya.reshape(bsz * t, W_A), yb, mod, p["g2"], final_g, p["w_out"], p["w_gate"], p["w_up"],
                  p["w_down"], seq=t, row0=row0, row_stride=row_stride,
                  tm=min(FFN_TM, bsz * t // MIN_STEPS))
    return out.reshape(bsz, t, D_MODEL), s_f, s_b


def kernel(x_prompt, x_sample, state_fwd, state_bwd, c, c_ctx, w_ada, b_ada, norm1_g, norm2_g, w_in, conv_w,
           a_log, dt_bias, dn_norm_g, sgu_ln_g, sgu_ln_b, sgu_w, sgu_b, w_out, w_gate, w_up, w_down, final_g):
    depth = w_ada.shape[0]
    assert depth == 1, "the final RMSNorm is fused into the last layer; only depth 1 is supported"
    n_lat = c.shape[0]
    assert 1 + n_lat <= MOD_ROWS
    cc = jnp.zeros((MOD_ROWS, D_MODEL), F32).at[0].set(c_ctx).at[1:1 + n_lat].set(c)
    s_zero = jnp.zeros((x_prompt.shape[0], H_A, DK, DV), F32)
    xp, xs = x_prompt, x_sample
    new_f, new_b = [], []
    gate_pad = jnp.zeros((2 * H_A,), F32)
    for l in range(depth):
        w_in_b = w_in[l].astype(BF16)
        n_main = 4 * W_A + 2 * W_B
        p = dict(
            g1=norm1_g[l].reshape(1, D_MODEL), g2=norm2_g[l].reshape(1, D_MODEL),
            w_main=w_in_b[:, :n_main],
            w_ba=jnp.pad(w_in_b[:, n_main:], ((0, 0), (0, GATE_W - 4 * H_A))),
            ln_g=sgu_ln_g[l].reshape(1, W_B), ln_b=sgu_ln_b[l].reshape(1, W_B),
            w_s=sgu_w[l].astype(BF16), b_s_t=jnp.transpose(sgu_b[l]),
            conv_w=conv_w[l],
            a_row=jnp.pad(jnp.concatenate([gate_pad, a_log[l].reshape(-1)]), (0, GATE_W - 4 * H_A)).reshape(1, GATE_W),
            dt_row=jnp.pad(jnp.concatenate([gate_pad, dt_bias[l].reshape(-1)]), (0, GATE_W - 4 * H_A)).reshape(1, GATE_W),
            dn_g=dn_norm_g[l].reshape(1, DV),
            w_out=w_out[l].astype(BF16), w_gate=w_gate[l].astype(BF16), w_up=w_up[l].astype(BF16),
            w_down=w_down[l].astype(BF16),
        )
        mod = _ada(cc, w_ada[l], b_ada[l]).reshape(MOD_ROWS, N_MOD, D_MODEL)
        fg = final_g.reshape(1, D_MODEL)
        xp, sf, sb = _trunk_layer(xp, mod, 0, 0, s_zero, s_zero, p, fg)
        new_f.append(sf)
        new_b.append(sb)
        xs, _, _ = _trunk_layer(xs, mod, 1, 1, state_fwd[:, l], state_bwd[:, l], p, fg)
    return (xp, xs, jnp.stack(new_f, axis=1), jnp.stack(new_b, axis=1))
```

```python
import functools

import jax
import jax.numpy as jnp
from jax import lax
from jax.experimental import pallas as pl
from jax.experimental.pallas import tpu as pltpu

F32 = jnp.float32
BF16 = jnp.bfloat16

D_MODEL = 1024
W_A = D_MODEL // 2
DK = 128
DV = 128
H_A = W_A // DV
W_B = D_MODEL - W_A
WG = 128
G_B = W_B // WG
SGU_CHUNK = 128
CONV_K = 5
DN_CHUNK = 128
D_FF = 2816
N_MOD = 6
EPS = 1e-6

MOD_ROWS = 16
GATE_W = 128
LANES = 128
INV_BASE = 16
FF_CHUNK = 256
IN_SUB = 256
IN_TM = 1024
FFN_SUB = 256
FFN_TM = 1024
MIN_STEPS = 8
DN_SHORT = 512
P1_BATCH = 8
VMEM_LIMIT = 56 * 1024 * 1024
VMEM_LIMIT_DN = 58 * 1024 * 1024


def _sigmoid(x):
    return 0.5 + 0.5 * jnp.tanh(0.5 * x)


def _silu(x):
    hx = 0.5 * x
    return hx + hx * jnp.tanh(hx)


def _gelu_tanh(x):
    c = 0.7978845608028654
    return 0.5 * x * (1.0 + jnp.tanh(c * (x + 0.044715 * (x * x * x))))


def _softplus(x):
    return jnp.maximum(x, 0.0) + jnp.log(1.0 + jnp.exp(-jnp.abs(x)))


def _dot(a, b):
    return jnp.dot(a, b, preferred_element_type=F32)


def _dot_nt(a, b):
    return lax.dot_general(a, b, (((1,), (1,)), ((), ())), preferred_element_type=F32)


def _dot_tn(a, b):
    return lax.dot_general(a, b, (((0,), (0,)), ((), ())), preferred_element_type=F32)


def _split3(x):
    hi = x.astype(BF16)
    r1 = x - hi.astype(F32)
    mid = r1.astype(BF16)
    lo = (r1 - mid.astype(F32)).astype(BF16)
    return hi, mid, lo


def _dot_exact_lhs(a_bf16, x):
    hi, mid, lo = _split3(x)
    return _dot(a_bf16, hi) + _dot(a_bf16, mid) + _dot(a_bf16, lo)


def _ada_kernel(c_ref, w_ref, b_ref, o_ref):
    s = _silu(c_ref[...])
    o_ref[...] = _dot(s.astype(BF16), w_ref[...].astype(BF16)) + b_ref[...]


def _ada(cc, w_ada, b_ada):
    n_out = w_ada.shape[1]
    bn = D_MODEL
    return pl.pallas_call(
        _ada_kernel,
        grid=(n_out // bn,),
        in_specs=[
            pl.BlockSpec((MOD_ROWS, D_MODEL), lambda j: (0, 0)),
            pl.BlockSpec((D_MODEL, bn), lambda j: (0, j)),
            pl.BlockSpec((1, bn), lambda j: (0, j)),
        ],
        out_specs=pl.BlockSpec((MOD_ROWS, bn), lambda j: (0, j)),
        out_shape=jax.ShapeDtypeStruct((MOD_ROWS, n_out), F32),
        compiler_params=pltpu.CompilerParams(vmem_limit_bytes=VMEM_LIMIT),
        name="ada",
    )(cc, w_ada, b_ada.reshape(1, n_out))


def _inproj_kernel(x_ref, mod_ref, g1_ref, w_ref, arow_ref, dtrow_ref, lng_ref, lnb_ref, ws_ref, bs_ref,
                   qkv_ref, z_ref, ba_ref, yb_ref, wm_ref, wba_ref, *, tm):
    n_main = 4 * W_A + 2 * W_B

    @pl.when(pl.program_id(0) == 0)
    def _():
        for c in range(0, n_main, W_A):
            wm_ref[:, c:c + W_A] = w_ref[:, c:c + W_A].astype(BF16)
        wba_ref[...] = jnp.zeros(wba_ref.shape, BF16)
        wba_ref[:, 0:4 * H_A] = w_ref[:, n_main:n_main + 4 * H_A].astype(BF16)

    subs = [slice(r, r + IN_SUB) for r in range(0, tm, IN_SUB)]
    ri = lax.broadcasted_iota(jnp.int32, (IN_SUB, IN_SUB), 0)
    ci = lax.broadcasted_iota(jnp.int32, (IN_SUB, IN_SUB), 1)
    same_chunk = (ri // DN_CHUNK) == (ci // DN_CHUNK)
    prefix_op = (same_chunk & (ri >= ci)).astype(BF16)

    def norm(r):
        x = x_ref[r, :]
        ms = jnp.mean(x * x, axis=-1, keepdims=True)
        xn = x * lax.rsqrt(ms + EPS) * g1_ref[...]
        return (xn * (1.0 + mod_ref[1:2, :]) + mod_ref[0:1, :]).astype(BF16)

    def matmul_tasks(r, hb):
        def qkv_piece(j):
            def run():
                l = slice(j * W_A, (j + 1) * W_A)
                qkv_ref[r, l] = _dot(hb, wm_ref[:, l]).astype(BF16)
            return run

        def z_piece():
            z_ref[r, :] = _dot(hb, wm_ref[:, 3 * W_A:4 * W_A]).astype(BF16)

        def gate_piece():
            logit = _dot(hb, wba_ref[...])
            lane = lax.broadcasted_iota(jnp.int32, logit.shape, 1)
            g_all = -jnp.exp(arow_ref[...]) * _softplus(logit + dtrow_ref[...])
            cum_f = _dot_exact_lhs(prefix_op, g_all)
            tot = jnp.concatenate(
                [jnp.broadcast_to(cum_f[c + DN_CHUNK - 1:c + DN_CHUNK, :], (DN_CHUNK, GATE_W))
                 for c in range(0, IN_SUB, DN_CHUNK)], axis=0)
            cum_b = tot - cum_f + g_all
            ba_ref[r, :] = jnp.where(lane < 2 * H_A, _sigmoid(logit), jnp.where(lane < 3 * H_A, cum_f, cum_b))

        return [qkv_piece(0), qkv_piece(1), qkv_piece(2), z_piece, gate_piece]

    def sgu_tasks(r, st):
        def gelu_u():
            st["u"] = _gelu_tanh(st["u"])

        def gelu_v():
            st["v"] = _gelu_tanh(st["v"])

        def layernorm_v():
            vb = st["v"]
            mu = jnp.mean(vb, axis=-1, keepdims=True)
            vc = vb - mu
            var = jnp.mean(vc * vc, axis=-1, keepdims=True)
            st["v"] = (vc * lax.rsqrt(var + EPS) * lng_ref[...] + lnb_ref[...]).astype(BF16)

        def mix(c):
            def run():
                rr = slice(c * SGU_CHUNK, (c + 1) * SGU_CHUNK)
                ro = slice(r.start + c * SGU_CHUNK, r.start + (c + 1) * SGU_CHUNK)
                for g in range(G_B):
                    l = slice(g * WG, (g + 1) * WG)
                    mixed = _dot(ws_ref[g], st["v"][rr, l]) + bs_ref[:, g:g + 1]
                    yb_ref[ro, l] = (st["u"][rr, l] * mixed).astype(BF16)
            return run

        return [gelu_u, gelu_v, layernorm_v] + [mix(c) for c in range(IN_SUB // SGU_CHUNK)]

    hb = norm(subs[0])
    for i, r in enumerate(subs):
        st = {"u": _dot(hb, wm_ref[:, 4 * W_A:4 * W_A + W_B]),
              "v": _dot(hb, wm_ref[:, 4 * W_A + W_B:4 * W_A + 2 * W_B])}
        mm = matmul_tasks(r, hb)
        ew = sgu_tasks(r, st)
        nxt = {}
        if i + 1 < len(subs):
            ew.append(lambda i=i: nxt.update(hb=norm(subs[i + 1])))
        for k in range(max(len(mm), len(ew))):
            if k < len(mm):
                mm[k]()
            if k < len(ew):
                ew[k]()
        hb = nxt.get("hb")


def _inproj(x2d, mod, g1, w_in, a_row, dt_row, ln_g, ln_b, w_s, b_s_t, *, seq, row0, row_stride, tm):
    n_tok = x2d.shape[0]
    assert row_stride == 0 or seq % tm == 0

    def mod_idx(i):
        return (row0 + ((i * tm) // seq) * row_stride, 0, 0)

    const2 = lambda i: (0, 0)
    tok = lambda i: (i, 0)
    return pl.pallas_call(
        functools.partial(_inproj_kernel, tm=tm),
        grid=(n_tok // tm,),
        in_specs=[
            pl.BlockSpec((tm, D_MODEL), tok),
            pl.BlockSpec((None, N_MOD, D_MODEL), mod_idx),
            pl.BlockSpec((1, D_MODEL), const2),
            pl.BlockSpec(w_in.shape, const2, pipeline_mode=pl.Buffered(1)),
            pl.BlockSpec((1, GATE_W), const2),
            pl.BlockSpec((1, GATE_W), const2),
            pl.BlockSpec((1, W_B), const2),
            pl.BlockSpec((1, W_B), const2),
            pl.BlockSpec(w_s.shape, lambda i: (0, 0, 0)),
            pl.BlockSpec(b_s_t.shape, const2),
        ],
        out_specs=[
            pl.BlockSpec((tm, 3 * W_A), tok),
            pl.BlockSpec((tm, W_A), tok),
            pl.BlockSpec((tm, GATE_W), tok),
            pl.BlockSpec((tm, W_B), tok),
        ],
        out_shape=[
            jax.ShapeDtypeStruct((n_tok, 3 * W_A), BF16),
            jax.ShapeDtypeStruct((n_tok, W_A), BF16),
            jax.ShapeDtypeStruct((n_tok, GATE_W), F32),
            jax.ShapeDtypeStruct((n_tok, W_B), BF16),
        ],
        scratch_shapes=[pltpu.VMEM((D_MODEL, 4 * W_A + 2 * W_B), BF16),
                        pltpu.VMEM((D_MODEL, GATE_W), BF16)],
        compiler_params=pltpu.CompilerParams(vmem_limit_bytes=VMEM_LIMIT),
        name="inproj",
    )(x2d, mod, g1, w_in, a_row, dt_row, ln_g, ln_b, w_s, b_s_t)


def _compress_diag_blocks(a):
    n = a.shape[0]
    lane = lax.broadcasted_iota(jnp.int32, (INV_BASE, n), 1)
    out = jnp.where((lane // INV_BASE) == 0, a[0:INV_BASE, :], 0.0)
    for rb in range(1, n // INV_BASE):
        out = out + jnp.where((lane // INV_BASE) == rb, a[rb * INV_BASE:(rb + 1) * INV_BASE, :], 0.0)
    return out


def _tri_inv_stages(get_diag, get_mats, out):
    n = DN_CHUNK
    nblk = n // INV_BASE
    per = LANES // n
    assert per >= 1

    def eliminate():
        diag_blocks = get_diag()
        assert len(diag_blocks) % per == 0
        lane1 = lax.broadcasted_iota(jnp.int32, (INV_BASE, n), 1)
        lane2 = lax.broadcasted_iota(jnp.int32, (INV_BASE, per * n), 1)
        sub2 = lax.broadcasted_iota(jnp.int32, (INV_BASE, per * n), 0)
        packed = [diag_blocks[p] if per == 1 else
                  jnp.concatenate([diag_blocks[per * p + q] for q in range(per)], axis=1)
                  for p in range(len(diag_blocks) // per)]
        eye = ((lane2 % INV_BASE) == sub2).astype(F32)
        xs = [eye for _ in packed]
        grp = (lane2 // INV_BASE) * INV_BASE
        for j in range(INV_BASE - 1):
            cols = [jnp.take_along_axis(nc, grp + j, axis=1) for nc in packed]
            xs = [x - c * x[j:j + 1, :] for x, c in zip(xs, cols)]

        def expand(xc):
            xc = xc.astype(BF16)
            zero_rows = jnp.zeros_like(xc)
            return jnp.concatenate([jnp.where((lane1 // INV_BASE) == rb, xc, zero_rows) for rb in range(nblk)],
                                   axis=0)

        out["inv"] = [expand(x[:, q * n:(q + 1) * n]) for x in xs for q in range(per)]

    def merge(bs):
        def run():
            row = lax.broadcasted_iota(jnp.int32, (n, n), 0)
            col = lax.broadcasted_iota(jnp.int32, (n, n), 1)
            zero = jnp.zeros((n, n), BF16)
            cmask = (((row // (2 * bs)) == (col // (2 * bs))) & (((row // bs) % 2) == 1)
                     & (((col // bs) % 2) == 0))
            ds = out["inv"]
            cs = [jnp.where(cmask, a, zero) for a in get_mats()]
            t1 = [_dot(d, c) for d, c in zip(ds, cs)]
            t2 = [_dot(t.astype(BF16), d) for t, d in zip(t1, ds)]
            out["inv"] = [jnp.where(cmask, (-t).astype(BF16), d) for d, t in zip(ds, t2)]
        return run

    levels = []
    bs = INV_BASE
    while bs < n:
        levels.append(merge(bs))
        bs *= 2
    return [eliminate] + levels


def _dn_kernel(q_ref, k_ref, v_ref, z_ref, ba_ref, cwq_ref, cwk_ref, cwv_ref,
               ng_ref, sf0_ref, sb0_ref,
               ya_ref, sf_ref, sb_ref,
               padq_ref, padk_ref, padv_ref, qs_ref, ks_ref, vs_ref,
               r_ref, mq_ref, et_ref, o_ref, *prep_refs, seq, hps):
    t = seq
    c_len = DN_CHUNK
    n = t // c_len
    step = pl.program_id(0)
    head0 = (jnp.maximum(step - 1, 0) % (H_A // hps)) * hps
    slot_new = step % 2
    slot_cur = 1 - slot_new
    rb = min(t, 256)
    heads = [slice(g * DK, (g + 1) * DK) for g in range(hps)]

    zeros8 = jnp.zeros((8, DK), F32)
    for src_ref, pad_ref in ((q_ref, padq_ref), (k_ref, padk_ref), (v_ref, padv_ref)):
        for g, l in enumerate(heads):
            pad_ref[g, 0:8, :] = zeros8
            pad_ref[g, 8 + t:16 + t, :] = zeros8
            for r in range(0, t, rb):
                pad_ref[g, 8 + r:8 + r + rb, :] = src_ref[r:r + rb, l].astype(F32)

    def conv_block(i):
        r0 = pl.multiple_of(i * c_len, c_len)
        for pad_ref, cw_ref, dst_ref, l2, scale in ((padq_ref, cwq_ref, qs_ref, True, DK ** -0.5),
                                                    (padk_ref, cwk_ref, ks_ref, True, 1.0),
                                                    (padv_ref, cwv_ref, vs_ref, False, 1.0)):
            for g, l in enumerate(heads):
                acc = cw_ref[0:1, l] * pad_ref[g, pl.ds(r0 + 6, c_len), :]
                for j in range(1, CONV_K):
                    acc = acc + cw_ref[j:j + 1, l] * pad_ref[g, pl.ds(r0 + 6 + j, c_len), :]
                y = _silu(acc)
                if l2:
                    y = y * (lax.rsqrt(jnp.sum(y * y, axis=-1, keepdims=True) + EPS) * scale)
                dst_ref[slot_new, g, pl.ds(r0, c_len), :] = y

    @pl.when(step == 0)
    def _():
        def body(i, carry):
            conv_block(i)
            return carry
        lax.fori_loop(0, n, body, 0)

    @pl.when(step > 0)
    def _():
        _dn_group(z_ref, ba_ref, ng_ref, sf0_ref, sb0_ref, ya_ref, sf_ref, sb_ref,
                  qs_ref.at[slot_cur], ks_ref.at[slot_cur], vs_ref.at[slot_cur],
                  r_ref, mq_ref, et_ref, o_ref, prep_refs, conv_block, head0, t, hps)


def _dn_group(z_ref, ba_ref, ng_ref, sf0_ref, sb0_ref, ya_ref, sf_ref, sb_ref, qs_ref, ks_ref, vs_ref,
              r_ref, mq_ref, et_ref, o_ref, prep_refs, conv_block, head0, t, hps):
    c_len = DN_CHUNK
    n = t // c_len
    rb = min(t, 256)

    row = lax.broadcasted_iota(jnp.int32, (c_len, c_len), 0)
    col = lax.broadcasted_iota(jnp.int32, (c_len, c_len), 1)
    lower_incl = row >= col
    upper_incl = row <= col
    lower_strict = row > col
    gate_lane = [jnp.full((c_len, GATE_W), head0 + g, jnp.int32) for g in range(hps)]
    nb = max(1, min(P1_BATCH // hps, n))
    assert n % nb == 0
    n_batch = n // nb
    chunks = [(g, i) for g in range(hps) for i in range(nb)]
    items = [(g, i, d) for g, i in chunks for d in range(2)]
    pa16_ref, pdiag_ref, prhs_ref, pqkd_ref, pkdt_ref, pqe_ref, pb_ref = prep_refs

    def rows(it, stride):
        return [pl.multiple_of((it * nb + i) * stride, stride) for i in range(nb)]

    def prep_tasks(it, slot):
        r64, r8 = rows(it, c_len), rows(it, 8)
        sh = {}

        def head():
            sh["q"] = {gi: qs_ref[gi[0], pl.ds(r64[gi[1]], c_len), :] for gi in chunks}
            sh["k"] = {gi: ks_ref[gi[0], pl.ds(r64[gi[1]], c_len), :] for gi in chunks}
            kb = {gi: sh["k"][gi].astype(BF16) for gi in chunks}
            sh["kk"] = {gi: _dot_nt(kb[gi], kb[gi]) for gi in chunks}
            sh["qk"] = {gi: _dot_nt(sh["q"][gi].astype(BF16), kb[gi]) for gi in chunks}

        def item(idx, g, i, d):
            def run():
                q, k = sh["q"][g, i], sh["k"][g, i]
                gates = ba_ref[pl.ds(r64[i], c_len), :]
                b = jnp.take_along_axis(gates, gate_lane[g] + d * H_A, axis=1)
                cum = jnp.take_along_axis(gates, gate_lane[g] + (2 + d) * H_A, axis=1)
                dm = cum[:, 0:c_len] - cum[:, 0:c_len].T
                dec_sys = jnp.where(lower_incl, jnp.exp(dm if d == 0 else -dm), 0.0)
                dec_qk = dec_sys if d == 0 else jnp.where(upper_incl, jnp.exp(dm), 0.0)
                a = jnp.where(lower_strict, b[:, 0:c_len] * sh["kk"][g, i] * dec_sys, 0.0)
                e_col = jnp.exp(cum)
                v = vs_ref[g, pl.ds(r64[i], c_len), :]
                if d == 0:
                    rhs = jnp.concatenate([v * b, k * (b * e_col)], axis=1)
                else:
                    rhs = jnp.concatenate([v, k * e_col], axis=1)
                    pb_ref[slot, idx // 2] = b
                tot = cum[c_len - 1:c_len, :] if d == 0 else cum[0:1, :]
                et_ref[g, d, pl.ds(r8[i], 8), :] = jnp.exp(jnp.broadcast_to(tot, (8, DK)))
                pa16_ref[slot, idx] = a.astype(BF16)
                pdiag_ref[slot, idx] = _compress_diag_blocks(a)
                prhs_ref[slot, idx] = rhs.astype(BF16)
                pqkd_ref[slot, idx] = (sh["qk"][g, i] * dec_qk).astype(BF16)
                pkdt_ref[slot, idx] = (k * jnp.exp(tot - cum)).T.astype(BF16)
                pqe_ref[slot, idx] = q * e_col
            return run

        return [head] + [item(idx, g, i, d) for idx, (g, i, d) in enumerate(items)]

    def solve_tasks(it, slot):
        r64, r128, r192 = rows(it, c_len), rows(it, DK), rows(it, DK + c_len)
        st = {}
        inverse = _tri_inv_stages(lambda: [pdiag_ref[slot, x] for x in range(len(items))],
                                  lambda: [pa16_ref[slot, x] for x in range(len(items))], st)

        def uw_stage():
            uw = []
            for idx, (g, i, d) in enumerate(items):
                if d == 0:
                    uw.append(_dot(st["inv"][idx], prhs_ref[slot, idx]))
                else:
                    b = pb_ref[slot, idx // 2]
                    uw.append(jnp.concatenate([b, b], axis=1) * _dot_tn(st["inv"][idx], prhs_ref[slot, idx]))
            st["uwb"] = [x.astype(BF16) for x in uw]

        def out_stage():
            kuw = [_dot(pkdt_ref[slot, x], u) for x, u in enumerate(st["uwb"])]
            quw = [_dot(pqkd_ref[slot, x], u) for x, u in enumerate(st["uwb"])]
            for idx, (g, i, d) in enumerate(items):
                r_ref[g, d, pl.ds(r128[i], DK), :] = kuw[idx][:, 0:DV]
                mq_ref[g, d, pl.ds(r192[i], DK), :] = (-kuw[idx][:, DV:DV + DK]).astype(BF16)
                mq_ref[g, d, pl.ds(r192[i] + DK, c_len), :] = (pqe_ref[slot, idx]
                                                               - quw[idx][:, DV:DV + DK]).astype(BF16)
                o_ref[g, d, pl.ds(r64[i], c_len), :] = quw[idx][:, 0:DV]

        return inverse + [uw_stage, out_stage]

    def emit(solve, prep):
        per = -(-len(prep) // max(len(solve), 1)) if solve else len(prep)
        p = 0
        for s in solve:
            for task in prep[p:p + per]:
                task()
            p += per
            s()
        for task in prep[p:]:
            task()

    emit([], prep_tasks(0, 0))

    def phase1(it, carry):
        slot = it % 2
        emit(solve_tasks(it, slot), prep_tasks(it + 1, 1 - slot))
        return carry

    lax.fori_loop(0, n_batch - 1, phase1, 0)
    emit(solve_tasks(n_batch - 1, (n_batch - 1) % 2), [])


    def phase2(i, carry):
        new = []
        for g in range(hps):
            for d in range(2):
                s = carry[2 * g + d]
                c = i if d == 0 else n - 1 - i
                r64 = pl.multiple_of(c * c_len, c_len)
                r128 = pl.multiple_of(c * DK, DK)
                r192 = pl.multiple_of(c * (DK + c_len), DK + c_len)
                ms = _dot(mq_ref[g, d, pl.ds(r192, DK + c_len), :], s.astype(BF16))
                et = et_ref[g, d, pl.ds(pl.multiple_of(c * 8, 8), 8), :]
                new.append(s * et[0:1, :] + r_ref[g, d, pl.ds(r128, DK), :] + ms[0:DK])
                o_ref[g, d, pl.ds(r64, c_len), :] = o_ref[g, d, pl.ds(r64, c_len), :] + ms[DK:DK + c_len]
        conv_block(i)
        return tuple(new)

    init = []
    for g in range(hps):
        init += [sf0_ref[g], sb0_ref[g]]
    fin = lax.fori_loop(0, n, phase2, tuple(init))
    for g in range(hps):
        sf_ref[g] = fin[2 * g]
        sb_ref[g] = fin[2 * g + 1]

    for g in range(hps):
        l = slice(g * DV, (g + 1) * DV)
        for r in range(0, t, rb):
            o = o_ref[g, 0, r:r + rb, :] + o_ref[g, 1, r:r + rb, :]
            ms = jnp.mean(o * o, axis=-1, keepdims=True)
            y = o * lax.rsqrt(ms + EPS) * ng_ref[...]
            ya_ref[r:r + rb, l] = (y * _silu(z_ref[r:r + rb, l].astype(F32))).astype(BF16)


def _deltanet(qkv, z, ba, conv_w, norm_g, s_f0, s_b0, *, hps):
    bsz, t, _ = qkv.shape
    n = t // DN_CHUNK
    hb = H_A // hps
    n_grp = bsz * hb
    prep = lambda s: jnp.minimum(s, n_grp - 1)
    run = lambda s: jnp.maximum(s - 1, 0)
    prep_col = lambda off: (lambda s: (prep(s) // hb, 0, off + prep(s) % hb))
    prep_cw = lambda off: (lambda s: (0, off + prep(s) % hb))
    run_col = lambda s: (run(s) // hb, 0, run(s) % hb)
    run_state = lambda s: (run(s) // hb, run(s) % hb, 0, 0)
    w = hps * DK
    n_items = 2 * hps * max(1, min(P1_BATCH // hps, n))
    return pl.pallas_call(
        functools.partial(_dn_kernel, seq=t, hps=hps),
        grid=(n_grp + 1,),
        in_specs=[
            pl.BlockSpec((None, t, w), prep_col(0)),
            pl.BlockSpec((None, t, w), prep_col(hb)),
            pl.BlockSpec((None, t, w), prep_col(2 * hb)),
            pl.BlockSpec((None, t, w), run_col),
            pl.BlockSpec((None, t, GATE_W), lambda s: (run(s) // hb, 0, 0)),
            pl.BlockSpec((CONV_K, w), prep_cw(0)),
            pl.BlockSpec((CONV_K, w), prep_cw(hb)),
            pl.BlockSpec((CONV_K, w), prep_cw(2 * hb)),
            pl.BlockSpec((1, DV), lambda s: (0, 0)),
            pl.BlockSpec((None, hps, DK, DV), run_state),
            pl.BlockSpec((None, hps, DK, DV), run_state),
        ],
        out_specs=[
            pl.BlockSpec((None, t, w), run_col),
            pl.BlockSpec((None, hps, DK, DV), run_state),
            pl.BlockSpec((None, hps, DK, DV), run_state),
        ],
        out_shape=[
            jax.ShapeDtypeStruct((bsz, t, W_A), BF16),
            jax.ShapeDtypeStruct((bsz, H_A, DK, DV), F32),
            jax.ShapeDtypeStruct((bsz, H_A, DK, DV), F32),
        ],
        scratch_shapes=[
            pltpu.VMEM((hps, t + 16, DK), F32),
            pltpu.VMEM((hps, t + 16, DK), F32),
            pltpu.VMEM((hps, t + 16, DV), F32),
            pltpu.VMEM((2, hps, t, DK), F32),
            pltpu.VMEM((2, hps, t, DK), F32),
            pltpu.VMEM((2, hps, t, DV), F32),
            pltpu.VMEM((hps, 2, n * DK, DV), F32),
            pltpu.VMEM((hps, 2, n * (DK + DN_CHUNK), DK), BF16),
            pltpu.VMEM((hps, 2, n * 8, DV), F32),
            pltpu.VMEM((hps, 2, t, DV), F32),
            pltpu.VMEM((2, n_items, DN_CHUNK, DN_CHUNK), BF16),
            pltpu.VMEM((2, n_items, INV_BASE, DN_CHUNK), F32),
            pltpu.VMEM((2, n_items, DN_CHUNK, DV + DK), BF16),
            pltpu.VMEM((2, n_items, DN_CHUNK, DN_CHUNK), BF16),
            pltpu.VMEM((2, n_items, DK, DN_CHUNK), BF16),
            pltpu.VMEM((2, n_items, DN_CHUNK, DK), F32),
            pltpu.VMEM((2, n_items // 2, DN_CHUNK, DK), F32),
        ],
        compiler_params=pltpu.CompilerParams(vmem_limit_bytes=VMEM_LIMIT_DN),
        name="deltanet",
    )(qkv, qkv, qkv, z, ba, conv_w, conv_w, conv_w, norm_g, s_f0, s_b0)


def _outffn_kernel(x_ref, ya_ref, yb_ref, mod_ref, n2_ref, fg_ref, wo_ref, wg_ref, wu_ref, wd_ref, o_ref,
                   act_ref, *, tm):
    subs = [slice(r, r + FFN_SUB) for r in range(0, tm, FFN_SUB)]
    hb = []
    for r in subs:
        y = _dot(ya_ref[r, :], wo_ref[0:W_A, :]) + _dot(yb_ref[r, :], wo_ref[W_A:D_MODEL, :])
        x1 = x_ref[r, :] + mod_ref[2:3, :] * y
        o_ref[r, :] = x1
        ms = jnp.mean(x1 * x1, axis=-1, keepdims=True)
        h = (x1 * lax.rsqrt(ms + EPS) * n2_ref[...]) * (1.0 + mod_ref[4:5, :]) + mod_ref[3:4, :]
        hb.append(h.astype(BF16))
    for c in range(D_FF // FF_CHUNK):
        l = slice(c * FF_CHUNK, (c + 1) * FF_CHUNK)
        for r, h in zip(subs, hb):
            act_ref[r, l] = (_silu(_dot(h, wg_ref[:, l])) * _dot(h, wu_ref[:, l])).astype(BF16)
    for r in subs:
        x2 = o_ref[r, :] + mod_ref[5:6, :] * _dot(act_ref[r, :], wd_ref[...])
        ms2 = jnp.mean(x2 * x2, axis=-1, keepdims=True)
        o_ref[r, :] = x2 * lax.rsqrt(ms2 + EPS) * fg_ref[...]


def _outffn(x2d, ya, yb, mod, n2, fg, w_out, w_gate, w_up, w_down, *, seq, row0, row_stride, tm):
    n_tok = x2d.shape[0]
    assert row_stride == 0 or seq % tm == 0

    def mod_idx(i):
        return (row0 + ((i * tm) // seq) * row_stride, 0, 0)

    const2 = lambda i: (0, 0)
    tok = lambda i: (i, 0)
    resident = lambda w: pl.BlockSpec(w.shape, const2, pipeline_mode=pl.Buffered(1))
    return pl.pallas_call(
        functools.partial(_outffn_kernel, tm=tm),
        grid=(n_tok // tm,),
        in_specs=[
            pl.BlockSpec((tm, D_MODEL), tok),
            pl.BlockSpec((tm, W_A), tok),
            pl.BlockSpec((tm, W_B), tok),
            pl.BlockSpec((None, N_MOD, D_MODEL), mod_idx),
            pl.BlockSpec((1, D_MODEL), const2),
            pl.BlockSpec((1, D_MODEL), const2),
            resident(w_out), resident(w_gate), resident(w_up), resident(w_down),
        ],
        out_specs=pl.BlockSpec((tm, D_MODEL), tok),
        out_shape=jax.ShapeDtypeStruct((n_tok, D_MODEL), F32),
        scratch_shapes=[pltpu.VMEM((tm, D_FF), BF16)],
        compiler_params=pltpu.CompilerParams(vmem_limit_bytes=VMEM_LIMIT),
        name="outffn",
    )(x2d, ya, yb, mod, n2, fg, w_out, w_gate, w_up, w_down)


def _trunk_layer(x, mod, row0, row_stride, s_f0, s_b0, p, final_g):
    bsz, t, _ = x.shape
    x2d = x.reshape(bsz * t, D_MODEL)
    qkv, z, ba, yb = _inproj(x2d, mod, p["g1"], p["w_in"], p["a_row"], p["dt_row"], p["ln_g"], p["ln_b"], p["w_s"],
                             p["b_s_t"], seq=t, row0=row0, row_stride=row_stride, tm=IN_TM)
    ya, s_f, s_b = _deltanet(qkv.reshape(bsz, t, 3 * W_A), z.reshape(bsz, t, W_A), ba.reshape(bsz, t, GATE_W),
                             p["conv_w"], p["dn_g"], s_f0, s_b0, hps=H_A if t <= DN_SHORT else 2)
    out = _outffn(x2d, ya.reshape(bsz * t, W_A), yb, mod, p["g2"], final_g, p["w_out"], p["w_gate"], p["w_up"],
                  p["w_down"], seq=t, row0=row0, row_stride=row_stride,
                  tm=min(FFN_TM, bsz * t // MIN_STEPS))
    return out.reshape(bsz, t, D_MODEL), s_f, s_b


def kernel(x_prompt, x_sample, state_fwd, state_bwd, c, c_ctx, w_ada, b_ada, norm1_g, norm2_g, w_in, conv_w,
           a_log, dt_bias, dn_norm_g, sgu_ln_g, sgu_ln_b, sgu_w, sgu_b, w_out, w_gate, w_up, w_down, final_g):
    depth = w_ada.shape[0]
    assert depth == 1, "the final RMSNorm is fused into the last layer; only depth 1 is supported"
    n_lat = c.shape[0]
    assert 1 + n_lat <= MOD_ROWS
    cc = jnp.zeros((MOD_ROWS, D_MODEL), F32).at[0].set(c_ctx).at[1:1 + n_lat].set(c)
    s_zero = jnp.zeros((x_prompt.shape[0], H_A, DK, DV), F32)
    xp, xs = x_prompt, x_sample
    new_f, new_b = [], []
    gate_pad = jnp.zeros((2 * H_A,), F32)
    for l in range(depth):
        p = dict(
            g1=norm1_g[l].reshape(1, D_MODEL), g2=norm2_g[l].reshape(1, D_MODEL),
            w_in=w_in[l],
            ln_g=sgu_ln_g[l].reshape(1, W_B), ln_b=sgu_ln_b[l].reshape(1, W_B),
            w_s=sgu_w[l].astype(BF16), b_s_t=jnp.transpose(sgu_b[l]),
            conv_w=conv_w[l],
            a_row=jnp.pad(jnp.concatenate([gate_pad, a_log[l].reshape(-1)]), (0, GATE_W - 4 * H_A)).reshape(1, GATE_W),
            dt_row=jnp.pad(jnp.concatenate([gate_pad, dt_bias[l].reshape(-1)]), (0, GATE_W - 4 * H_A)).reshape(1, GATE_W),
            dn_g=dn_norm_g[l].reshape(1, DV),
            w_out=w_out[l].astype(BF16), w_gate=w_gate[l].astype(BF16), w_up=w_up[l].astype(BF16),
            w_down=w_down[l].astype(BF16),
        )
        mod = _ada(cc, w_ada[l], b_ada[l]).reshape(MOD_ROWS, N_MOD, D_MODEL)
        fg = final_g.reshape(1, D_MODEL)
        xp, sf, sb = _trunk_layer(xp, mod, 0, 0, s_zero, s_zero, p, fg)
        new_f.append(sf)
        new_b.append(sb)
        xs, _, _ = _trunk_layer(xs, mod, 1, 1, state_fwd[:, l], state_bwd[:, l], p, fg)
    return (xp, xs, jnp.stack(new_f, axis=1), jnp.stack(new_b, axis=1))
```

```python
import functools

import jax
import jax.numpy as jnp
from jax import lax
from jax.experimental import pallas as pl
from jax.experimental.pallas import tpu as pltpu

F32 = jnp.float32
BF16 = jnp.bfloat16

D_MODEL = 1024
W_A = D_MODEL // 2
DK = 128
DV = 128
H_A = W_A // DV
W_B = D_MODEL - W_A
WG = 128
G_B = W_B // WG
SGU_CHUNK = 128
CONV_K = 5
DN_CHUNK = 128
D_FF = 2816
N_MOD = 6
EPS = 1e-6

MOD_ROWS = 16
GATE_W = 128
LANES = 128
INV_BASE = 16
FF_CHUNK = 256
IN_SUB = 256
IN_TM = 1024
FFN_SUB = 256
FFN_TM = 1024
MIN_STEPS = 8
DN_SHORT = 512
P1_BATCH = 8
VMEM_LIMIT = 56 * 1024 * 1024
VMEM_LIMIT_DN = 58 * 1024 * 1024


def _sigmoid(x):
    return 0.5 + 0.5 * jnp.tanh(0.5 * x)


def _silu(x):
    hx = 0.5 * x
    return hx + hx * jnp.tanh(hx)


def _gelu_tanh(x):
    c = 0.7978845608028654
    return 0.5 * x * (1.0 + jnp.tanh(c * (x + 0.044715 * (x * x * x))))


def _softplus(x):
    return jnp.maximum(x, 0.0) + jnp.log(1.0 + jnp.exp(-jnp.abs(x)))


def _dot(a, b):
    return jnp.dot(a, b, preferred_element_type=F32)


def _dot_nt(a, b):
    return lax.dot_general(a, b, (((1,), (1,)), ((), ())), preferred_element_type=F32)


def _dot_tn(a, b):
    return lax.dot_general(a, b, (((0,), (0,)), ((), ())), preferred_element_type=F32)


def _split3(x):
    hi = x.astype(BF16)
    r1 = x - hi.astype(F32)
    mid = r1.astype(BF16)
    lo = (r1 - mid.astype(F32)).astype(BF16)
    return hi, mid, lo


def _dot_exact_lhs(a_bf16, x):
    hi, mid, lo = _split3(x)
    return _dot(a_bf16, hi) + _dot(a_bf16, mid) + _dot(a_bf16, lo)


def _ada_kernel(c_ref, w_ref, b_ref, o_ref):
    s = _silu(c_ref[...])
    o_ref[...] = _dot(s.astype(BF16), w_ref[...].astype(BF16)) + b_ref[...]


def _ada(cc, w_ada, b_ada, layer):
    n_out = w_ada.shape[2]
    bn = D_MODEL
    return pl.pallas_call(
        _ada_kernel,
        grid=(n_out // bn,),
        in_specs=[
            pl.BlockSpec((MOD_ROWS, D_MODEL), lambda j: (0, 0)),
            pl.BlockSpec((None, D_MODEL, bn), lambda j: (layer, 0, j)),
            pl.BlockSpec((1, bn), lambda j: (0, j)),
        ],
        out_specs=pl.BlockSpec((MOD_ROWS, bn), lambda j: (0, j)),
        out_shape=jax.ShapeDtypeStruct((MOD_ROWS, n_out), F32),
        compiler_params=pltpu.CompilerParams(vmem_limit_bytes=VMEM_LIMIT),
        name="ada",
    )(cc, w_ada, b_ada.reshape(1, n_out))


def _inproj_kernel(x_ref, mod_ref, g1_ref, w_ref, arow_ref, dtrow_ref, lng_ref, lnb_ref, ws_ref, bs_ref,
                   qkv_ref, z_ref, ba_ref, yb_ref, wm_ref, wba_ref, *, tm):
    n_main = 4 * W_A + 2 * W_B

    @pl.when(pl.program_id(0) == 0)
    def _():
        for c in range(0, n_main, W_A):
            wm_ref[:, c:c + W_A] = w_ref[:, c:c + W_A].astype(BF16)
        wba_ref[...] = jnp.zeros(wba_ref.shape, BF16)
        wba_ref[:, 0:4 * H_A] = w_ref[:, n_main:n_main + 4 * H_A].astype(BF16)

    subs = [slice(r, r + IN_SUB) for r in range(0, tm, IN_SUB)]
    ri = lax.broadcasted_iota(jnp.int32, (IN_SUB, IN_SUB), 0)
    ci = lax.broadcasted_iota(jnp.int32, (IN_SUB, IN_SUB), 1)
    same_chunk = (ri // DN_CHUNK) == (ci // DN_CHUNK)
    prefix_op = (same_chunk & (ri >= ci)).astype(BF16)

    def norm(r):
        x = x_ref[r, :]
        ms = jnp.mean(x * x, axis=-1, keepdims=True)
        xn = x * lax.rsqrt(ms + EPS) * g1_ref[...]
        return (xn * (1.0 + mod_ref[1:2, :]) + mod_ref[0:1, :]).astype(BF16)

    def matmul_tasks(r, hb):
        def qkv_piece(j):
            def run():
                l = slice(j * W_A, (j + 1) * W_A)
                qkv_ref[r, l] = _dot(hb, wm_ref[:, l]).astype(BF16)
            return run

        def z_piece():
            z_ref[r, :] = _dot(hb, wm_ref[:, 3 * W_A:4 * W_A]).astype(BF16)

        def gate_piece():
            logit = _dot(hb, wba_ref[...])
            lane = lax.broadcasted_iota(jnp.int32, logit.shape, 1)
            g_all = -jnp.exp(arow_ref[...]) * _softplus(logit + dtrow_ref[...])
            cum_f = _dot_exact_lhs(prefix_op, g_all)
            tot = jnp.concatenate(
                [jnp.broadcast_to(cum_f[c + DN_CHUNK - 1:c + DN_CHUNK, :], (DN_CHUNK, GATE_W))
                 for c in range(0, IN_SUB, DN_CHUNK)], axis=0)
            cum_b = tot - cum_f + g_all
            ba_ref[r, :] = jnp.where(lane < 2 * H_A, _sigmoid(logit), jnp.where(lane < 3 * H_A, cum_f, cum_b))

        return [qkv_piece(0), qkv_piece(1), qkv_piece(2), z_piece, gate_piece]

    def sgu_tasks(r, st):
        def gelu_u():
            st["u"] = _gelu_tanh(st["u"])

        def gelu_v():
            st["v"] = _gelu_tanh(st["v"])

        def layernorm_v():
            vb = st["v"]
            mu = jnp.mean(vb, axis=-1, keepdims=True)
            vc = vb - mu
            var = jnp.mean(vc * vc, axis=-1, keepdims=True)
            st["v"] = (vc * lax.rsqrt(var + EPS) * lng_ref[...] + lnb_ref[...]).astype(BF16)

        def mix(c):
            def run():
                rr = slice(c * SGU_CHUNK, (c + 1) * SGU_CHUNK)
                ro = slice(r.start + c * SGU_CHUNK, r.start + (c + 1) * SGU_CHUNK)
                for g in range(G_B):
                    l = slice(g * WG, (g + 1) * WG)
                    mixed = _dot(ws_ref[g], st["v"][rr, l]) + bs_ref[:, g:g + 1]
                    yb_ref[ro, l] = (st["u"][rr, l] * mixed).astype(BF16)
            return run

        return [gelu_u, gelu_v, layernorm_v] + [mix(c) for c in range(IN_SUB // SGU_CHUNK)]

    hb = norm(subs[0])
    for i, r in enumerate(subs):
        st = {"u": _dot(hb, wm_ref[:, 4 * W_A:4 * W_A + W_B]),
              "v": _dot(hb, wm_ref[:, 4 * W_A + W_B:4 * W_A + 2 * W_B])}
        mm = matmul_tasks(r, hb)
        ew = sgu_tasks(r, st)
        nxt = {}
        if i + 1 < len(subs):
            ew.append(lambda i=i: nxt.update(hb=norm(subs[i + 1])))
        for k in range(max(len(mm), len(ew))):
            if k < len(mm):
                mm[k]()
            if k < len(ew):
                ew[k]()
        hb = nxt.get("hb")


def _inproj(x2d, mod, g1, w_in, a_row, dt_row, ln_g, ln_b, w_s, b_s_t, *, layer, seq, row0, row_stride, tm):
    n_tok = x2d.shape[0]
    assert row_stride == 0 or seq % tm == 0

    def mod_idx(i):
        return (row0 + ((i * tm) // seq) * row_stride, 0, 0)

    const2 = lambda i: (0, 0)
    tok = lambda i: (i, 0)
    return pl.pallas_call(
        functools.partial(_inproj_kernel, tm=tm),
        grid=(n_tok // tm,),
        in_specs=[
            pl.BlockSpec((tm, D_MODEL), tok),
            pl.BlockSpec((None, N_MOD, D_MODEL), mod_idx),
            pl.BlockSpec((1, D_MODEL), const2),
            pl.BlockSpec((None,) + w_in.shape[1:], lambda i: (layer, 0, 0), pipeline_mode=pl.Buffered(1)),
            pl.BlockSpec((1, GATE_W), const2),
            pl.BlockSpec((1, GATE_W), const2),
            pl.BlockSpec((1, W_B), const2),
            pl.BlockSpec((1, W_B), const2),
            pl.BlockSpec(w_s.shape, lambda i: (0, 0, 0)),
            pl.BlockSpec(b_s_t.shape, const2),
        ],
        out_specs=[
            pl.BlockSpec((tm, 3 * W_A), tok),
            pl.BlockSpec((tm, W_A), tok),
            pl.BlockSpec((tm, GATE_W), tok),
            pl.BlockSpec((tm, W_B), tok),
        ],
        out_shape=[
            jax.ShapeDtypeStruct((n_tok, 3 * W_A), BF16),
            jax.ShapeDtypeStruct((n_tok, W_A), BF16),
            jax.ShapeDtypeStruct((n_tok, GATE_W), F32),
            jax.ShapeDtypeStruct((n_tok, W_B), BF16),
        ],
        scratch_shapes=[pltpu.VMEM((D_MODEL, 4 * W_A + 2 * W_B), BF16),
                        pltpu.VMEM((D_MODEL, GATE_W), BF16)],
        compiler_params=pltpu.CompilerParams(vmem_limit_bytes=VMEM_LIMIT),
        name="inproj",
    )(x2d, mod, g1, w_in, a_row, dt_row, ln_g, ln_b, w_s, b_s_t)


def _compress_diag_blocks(a):
    n = a.shape[0]
    lane = lax.broadcasted_iota(jnp.int32, (INV_BASE, n), 1)
    out = jnp.where((lane // INV_BASE) == 0, a[0:INV_BASE, :], 0.0)
    for rb in range(1, n // INV_BASE):
        out = out + jnp.where((lane // INV_BASE) == rb, a[rb * INV_BASE:(rb + 1) * INV_BASE, :], 0.0)
    return out


def _tri_inv_stages(get_diag, get_mats, out):
    n = DN_CHUNK
    nblk = n // INV_BASE
    per = LANES // n
    assert per >= 1

    def eliminate():
        diag_blocks = get_diag()
        assert len(diag_blocks) % per == 0
        lane1 = lax.broadcasted_iota(jnp.int32, (INV_BASE, n), 1)
        lane2 = lax.broadcasted_iota(jnp.int32, (INV_BASE, per * n), 1)
        sub2 = lax.broadcasted_iota(jnp.int32, (INV_BASE, per * n), 0)
        packed = [diag_blocks[p] if per == 1 else
                  jnp.concatenate([diag_blocks[per * p + q] for q in range(per)], axis=1)
                  for p in range(len(diag_blocks) // per)]
        eye = ((lane2 % INV_BASE) == sub2).astype(F32)
        xs = [eye for _ in packed]
        grp = (lane2 // INV_BASE) * INV_BASE
        for j in range(INV_BASE - 1):
            cols = [jnp.take_along_axis(nc, grp + j, axis=1) for nc in packed]
            xs = [x - c * x[j:j + 1, :] for x, c in zip(xs, cols)]

        def expand(xc):
            xc = xc.astype(BF16)
            zero_rows = jnp.zeros_like(xc)
            return jnp.concatenate([jnp.where((lane1 // INV_BASE) == rb, xc, zero_rows) for rb in range(nblk)],
                                   axis=0)

        out["inv"] = [expand(x[:, q * n:(q + 1) * n]) for x in xs for q in range(per)]

    def merge(bs):
        def run():
            row = lax.broadcasted_iota(jnp.int32, (n, n), 0)
            col = lax.broadcasted_iota(jnp.int32, (n, n), 1)
            zero = jnp.zeros((n, n), BF16)
            cmask = (((row // (2 * bs)) == (col // (2 * bs))) & (((row // bs) % 2) == 1)
                     & (((col // bs) % 2) == 0))
            ds = out["inv"]
            cs = [jnp.where(cmask, a, zero) for a in get_mats()]
            t1 = [_dot(d, c) for d, c in zip(ds, cs)]
            t2 = [_dot(t.astype(BF16), d) for t, d in zip(t1, ds)]
            out["inv"] = [jnp.where(cmask, (-t).astype(BF16), d) for d, t in zip(ds, t2)]
        return run

    levels = []
    bs = INV_BASE
    while bs < n:
        levels.append(merge(bs))
        bs *= 2
    return [eliminate] + levels


def _dn_kernel(q_ref, k_ref, v_ref, z_ref, ba_ref, cwq_ref, cwk_ref, cwv_ref,
               ng_ref, sf0_ref, sb0_ref,
               ya_ref, sf_ref, sb_ref,
               padq_ref, padk_ref, padv_ref, qs_ref, ks_ref, vs_ref,
               r_ref, mq_ref, et_ref, o_ref, *prep_refs, seq, hps):
    t = seq
    c_len = DN_CHUNK
    n = t // c_len
    step = pl.program_id(0)
    head0 = (jnp.maximum(step - 1, 0) % (H_A // hps)) * hps
    slot_new = step % 2
    slot_cur = 1 - slot_new
    rb = min(t, 256)
    heads = [slice(g * DK, (g + 1) * DK) for g in range(hps)]

    zeros8 = jnp.zeros((8, DK), F32)
    for src_ref, pad_ref in ((q_ref, padq_ref), (k_ref, padk_ref), (v_ref, padv_ref)):
        for g, l in enumerate(heads):
            pad_ref[g, 0:8, :] = zeros8
            pad_ref[g, 8 + t:16 + t, :] = zeros8
            for r in range(0, t, rb):
                pad_ref[g, 8 + r:8 + r + rb, :] = src_ref[r:r + rb, l].astype(F32)

    def conv_block(i):
        r0 = pl.multiple_of(i * c_len, c_len)
        for pad_ref, cw_ref, dst_ref, l2, scale in ((padq_ref, cwq_ref, qs_ref, True, DK ** -0.5),
                                                    (padk_ref, cwk_ref, ks_ref, True, 1.0),
                                                    (padv_ref, cwv_ref, vs_ref, False, 1.0)):
            for g, l in enumerate(heads):
                acc = cw_ref[0:1, l] * pad_ref[g, pl.ds(r0 + 6, c_len), :]
                for j in range(1, CONV_K):
                    acc = acc + cw_ref[j:j + 1, l] * pad_ref[g, pl.ds(r0 + 6 + j, c_len), :]
                y = _silu(acc)
                if l2:
                    y = y * (lax.rsqrt(jnp.sum(y * y, axis=-1, keepdims=True) + EPS) * scale)
                dst_ref[slot_new, g, pl.ds(r0, c_len), :] = y

    @pl.when(step == 0)
    def _():
        def body(i, carry):
            conv_block(i)
            return carry
        lax.fori_loop(0, n, body, 0)

    @pl.when(step > 0)
    def _():
        _dn_group(z_ref, ba_ref, ng_ref, sf0_ref, sb0_ref, ya_ref, sf_ref, sb_ref,
                  qs_ref.at[slot_cur], ks_ref.at[slot_cur], vs_ref.at[slot_cur],
                  r_ref, mq_ref, et_ref, o_ref, prep_refs, conv_block, head0, t, hps)


def _dn_group(z_ref, ba_ref, ng_ref, sf0_ref, sb0_ref, ya_ref, sf_ref, sb_ref, qs_ref, ks_ref, vs_ref,
              r_ref, mq_ref, et_ref, o_ref, prep_refs, conv_block, head0, t, hps):
    c_len = DN_CHUNK
    n = t // c_len
    rb = min(t, 256)

    row = lax.broadcasted_iota(jnp.int32, (c_len, c_len), 0)
    col = lax.broadcasted_iota(jnp.int32, (c_len, c_len), 1)
    lower_incl = row >= col
    upper_incl = row <= col
    lower_strict = row > col
    gate_lane = [jnp.full((c_len, GATE_W), head0 + g, jnp.int32) for g in range(hps)]
    nb = max(1, min(P1_BATCH // hps, n))
    assert n % nb == 0
    n_batch = n // nb
    chunks = [(g, i) for g in range(hps) for i in range(nb)]
    items = [(g, i, d) for g, i in chunks for d in range(2)]
    pa16_ref, pdiag_ref, prhs_ref, pqkd_ref, pkdt_ref, pqe_ref, pb_ref = prep_refs

    def rows(it, stride):
        return [pl.multiple_of((it * nb + i) * stride, stride) for i in range(nb)]

    def prep_tasks(it, slot):
        r64, r8 = rows(it, c_len), rows(it, 8)
        sh = {}

        def head():
            sh["q"] = {gi: qs_ref[gi[0], pl.ds(r64[gi[1]], c_len), :] for gi in chunks}
            sh["k"] = {gi: ks_ref[gi[0], pl.ds(r64[gi[1]], c_len), :] for gi in chunks}
            kb = {gi: sh["k"][gi].astype(BF16) for gi in chunks}
            sh["kk"] = {gi: _dot_nt(kb[gi], kb[gi]) for gi in chunks}
            sh["qk"] = {gi: _dot_nt(sh["q"][gi].astype(BF16), kb[gi]) for gi in chunks}

        def item(idx, g, i, d):
            def run():
                q, k = sh["q"][g, i], sh["k"][g, i]
                gates = ba_ref[pl.ds(r64[i], c_len), :]
                b = jnp.take_along_axis(gates, gate_lane[g] + d * H_A, axis=1)
                cum = jnp.take_along_axis(gates, gate_lane[g] + (2 + d) * H_A, axis=1)
                dm = cum[:, 0:c_len] - cum[:, 0:c_len].T
                dec_sys = jnp.where(lower_incl, jnp.exp(dm if d == 0 else -dm), 0.0)
                dec_qk = dec_sys if d == 0 else jnp.where(upper_incl, jnp.exp(dm), 0.0)
                a = jnp.where(lower_strict, b[:, 0:c_len] * sh["kk"][g, i] * dec_sys, 0.0)
                e_col = jnp.exp(cum)
                v = vs_ref[g, pl.ds(r64[i], c_len), :]
                if d == 0:
                    rhs = jnp.concatenate([v * b, k * (b * e_col)], axis=1)
                else:
                    rhs = jnp.concatenate([v, k * e_col], axis=1)
                    pb_ref[slot, idx // 2] = b
                tot = cum[c_len - 1:c_len, :] if d == 0 else cum[0:1, :]
                et_ref[g, d, pl.ds(r8[i], 8), :] = jnp.exp(jnp.broadcast_to(tot, (8, DK)))
                pa16_ref[slot, idx] = a.astype(BF16)
                pdiag_ref[slot, idx] = _compress_diag_blocks(a)
                prhs_ref[slot, idx] = rhs.astype(BF16)
                pqkd_ref[slot, idx] = (sh["qk"][g, i] * dec_qk).astype(BF16)
                pkdt_ref[slot, idx] = (k * jnp.exp(tot - cum)).T.astype(BF16)
                pqe_ref[slot, idx] = q * e_col
            return run

        return [head] + [item(idx, g, i, d) for idx, (g, i, d) in enumerate(items)]

    def solve_tasks(it, slot):
        r64, r128, r192 = rows(it, c_len), rows(it, DK), rows(it, DK + c_len)
        st = {}
        inverse = _tri_inv_stages(lambda: [pdiag_ref[slot, x] for x in range(len(items))],
                                  lambda: [pa16_ref[slot, x] for x in range(len(items))], st)

        def uw_stage():
            uw = []
            for idx, (g, i, d) in enumerate(items):
                if d == 0:
                    uw.append(_dot(st["inv"][idx], prhs_ref[slot, idx]))
                else:
                    b = pb_ref[slot, idx // 2]
                    uw.append(jnp.concatenate([b, b], axis=1) * _dot_tn(st["inv"][idx], prhs_ref[slot, idx]))
            st["uwb"] = [x.astype(BF16) for x in uw]

        def out_stage():
            kuw = [_dot(pkdt_ref[slot, x], u) for x, u in enumerate(st["uwb"])]
            quw = [_dot(pqkd_ref[slot, x], u) for x, u in enumerate(st["uwb"])]
            for idx, (g, i, d) in enumerate(items):
                r_ref[g, d, pl.ds(r128[i], DK), :] = kuw[idx][:, 0:DV]
                mq_ref[g, d, pl.ds(r192[i], DK), :] = (-kuw[idx][:, DV:DV + DK]).astype(BF16)
                mq_ref[g, d, pl.ds(r192[i] + DK, c_len), :] = (pqe_ref[slot, idx]
                                                               - quw[idx][:, DV:DV + DK]).astype(BF16)
                o_ref[g, d, pl.ds(r64[i], c_len), :] = quw[idx][:, 0:DV]

        return inverse + [uw_stage, out_stage]

    def emit(solve, prep):
        per = -(-len(prep) // max(len(solve), 1)) if solve else len(prep)
        p = 0
        for s in solve:
            for task in prep[p:p + per]:
                task()
            p += per
            s()
        for task in prep[p:]:
            task()

    emit([], prep_tasks(0, 0))

    def phase1(it, carry):
        slot = it % 2
        emit(solve_tasks(it, slot), prep_tasks(it + 1, 1 - slot))
        return carry

    lax.fori_loop(0, n_batch - 1, phase1, 0)
    emit(solve_tasks(n_batch - 1, (n_batch - 1) % 2), [])


    def phase2(i, carry):
        new = []
        for g in range(hps):
            for d in range(2):
                s = carry[2 * g + d]
                c = i if d == 0 else n - 1 - i
                r64 = pl.multiple_of(c * c_len, c_len)
                r128 = pl.multiple_of(c * DK, DK)
                r192 = pl.multiple_of(c * (DK + c_len), DK + c_len)
                ms = _dot(mq_ref[g, d, pl.ds(r192, DK + c_len), :], s.astype(BF16))
                et = et_ref[g, d, pl.ds(pl.multiple_of(c * 8, 8), 8), :]
                new.append(s * et[0:1, :] + r_ref[g, d, pl.ds(r128, DK), :] + ms[0:DK])
                o_ref[g, d, pl.ds(r64, c_len), :] = o_ref[g, d, pl.ds(r64, c_len), :] + ms[DK:DK + c_len]
        conv_block(i)
        return tuple(new)

    init = []
    for g in range(hps):
        init += [sf0_ref[g], sb0_ref[g]]
    fin = lax.fori_loop(0, n, phase2, tuple(init))
    for g in range(hps):
        sf_ref[g] = fin[2 * g]
        sb_ref[g] = fin[2 * g + 1]

    for g in range(hps):
        l = slice(g * DV, (g + 1) * DV)
        for r in range(0, t, rb):
            o = o_ref[g, 0, r:r + rb, :] + o_ref[g, 1, r:r + rb, :]
            ms = jnp.mean(o * o, axis=-1, keepdims=True)
            y = o * lax.rsqrt(ms + EPS) * ng_ref[...]
            ya_ref[r:r + rb, l] = (y * _silu(z_ref[r:r + rb, l].astype(F32))).astype(BF16)


def _deltanet(qkv, z, ba, conv_w, norm_g, s_f0, s_b0, *, hps):
    bsz, t, _ = qkv.shape
    n = t // DN_CHUNK
    hb = H_A // hps
    n_grp = bsz * hb
    prep = lambda s: jnp.minimum(s, n_grp - 1)
    run = lambda s: jnp.maximum(s - 1, 0)
    prep_col = lambda off: (lambda s: (prep(s) // hb, 0, off + prep(s) % hb))
    prep_cw = lambda off: (lambda s: (0, off + prep(s) % hb))
    run_col = lambda s: (run(s) // hb, 0, run(s) % hb)
    run_state = lambda s: (run(s) // hb, run(s) % hb, 0, 0)
    w = hps * DK
    n_items = 2 * hps * max(1, min(P1_BATCH // hps, n))
    return pl.pallas_call(
        functools.partial(_dn_kernel, seq=t, hps=hps),
        grid=(n_grp + 1,),
        in_specs=[
            pl.BlockSpec((None, t, w), prep_col(0)),
            pl.BlockSpec((None, t, w), prep_col(hb)),
            pl.BlockSpec((None, t, w), prep_col(2 * hb)),
            pl.BlockSpec((None, t, w), run_col),
            pl.BlockSpec((None, t, GATE_W), lambda s: (run(s) // hb, 0, 0)),
            pl.BlockSpec((CONV_K, w), prep_cw(0)),
            pl.BlockSpec((CONV_K, w), prep_cw(hb)),
            pl.BlockSpec((CONV_K, w), prep_cw(2 * hb)),
            pl.BlockSpec((1, DV), lambda s: (0, 0)),
            pl.BlockSpec((None, hps, DK, DV), run_state),
            pl.BlockSpec((None, hps, DK, DV), run_state),
        ],
        out_specs=[
            pl.BlockSpec((None, t, w), run_col),
            pl.BlockSpec((None, hps, DK, DV), run_state),
            pl.BlockSpec((None, hps, DK, DV), run_state),
        ],
        out_shape=[
            jax.ShapeDtypeStruct((bsz, t, W_A), BF16),
            jax.ShapeDtypeStruct((bsz, H_A, DK, DV), F32),
            jax.ShapeDtypeStruct((bsz, H_A, DK, DV), F32),
        ],
        scratch_shapes=[
            pltpu.VMEM((hps, t + 16, DK), F32),
            pltpu.VMEM((hps, t + 16, DK), F32),
            pltpu.VMEM((hps, t + 16, DV), F32),
            pltpu.VMEM((2, hps, t, DK), F32),
            pltpu.VMEM((2, hps, t, DK), F32),
            pltpu.VMEM((2, hps, t, DV), F32),
            pltpu.VMEM((hps, 2, n * DK, DV), F32),
            pltpu.VMEM((hps, 2, n * (DK + DN_CHUNK), DK), BF16),
            pltpu.VMEM((hps, 2, n * 8, DV), F32),
            pltpu.VMEM((hps, 2, t, DV), F32),
            pltpu.VMEM((2, n_items, DN_CHUNK, DN_CHUNK), BF16),
            pltpu.VMEM((2, n_items, INV_BASE, DN_CHUNK), F32),
            pltpu.VMEM((2, n_items, DN_CHUNK, DV + DK), BF16),
            pltpu.VMEM((2, n_items, DN_CHUNK, DN_CHUNK), BF16),
            pltpu.VMEM((2, n_items, DK, DN_CHUNK), BF16),
            pltpu.VMEM((2, n_items, DN_CHUNK, DK), F32),
            pltpu.VMEM((2, n_items // 2, DN_CHUNK, DK), F32),
        ],
        compiler_params=pltpu.CompilerParams(vmem_limit_bytes=VMEM_LIMIT_DN),
        name="deltanet",
    )(qkv, qkv, qkv, z, ba, conv_w, conv_w, conv_w, norm_g, s_f0, s_b0)


def _outffn_kernel(x_ref, ya_ref, yb_ref, mod_ref, n2_ref, fg_ref, wo_ref, wg_ref, wu_ref, wd_ref, o_ref,
                   act_ref, *, tm):
    subs = [slice(r, r + FFN_SUB) for r in range(0, tm, FFN_SUB)]
    hb = []
    for r in subs:
        y = _dot(ya_ref[r, :], wo_ref[0:W_A, :]) + _dot(yb_ref[r, :], wo_ref[W_A:D_MODEL, :])
        x1 = x_ref[r, :] + mod_ref[2:3, :] * y
        o_ref[r, :] = x1
        ms = jnp.mean(x1 * x1, axis=-1, keepdims=True)
        h = (x1 * lax.rsqrt(ms + EPS) * n2_ref[...]) * (1.0 + mod_ref[4:5, :]) + mod_ref[3:4, :]
        hb.append(h.astype(BF16))
    for c in range(D_FF // FF_CHUNK):
        l = slice(c * FF_CHUNK, (c + 1) * FF_CHUNK)
        for r, h in zip(subs, hb):
            act_ref[r, l] = (_silu(_dot(h, wg_ref[:, l])) * _dot(h, wu_ref[:, l])).astype(BF16)
    for r in subs:
        x2 = o_ref[r, :] + mod_ref[5:6, :] * _dot(act_ref[r, :], wd_ref[...])
        ms2 = jnp.mean(x2 * x2, axis=-1, keepdims=True)
        o_ref[r, :] = x2 * lax.rsqrt(ms2 + EPS) * fg_ref[...]


def _outffn(x2d, ya, yb, mod, n2, fg, w_out, w_gate, w_up, w_down, *, seq, row0, row_stride, tm):
    n_tok = x2d.shape[0]
    assert row_stride == 0 or seq % tm == 0

    def mod_idx(i):
        return (row0 + ((i * tm) // seq) * row_stride, 0, 0)

    const2 = lambda i: (0, 0)
    tok = lambda i: (i, 0)
    resident = lambda w: pl.BlockSpec(w.shape, const2, pipeline_mode=pl.Buffered(1))
    return pl.pallas_call(
        functools.partial(_outffn_kernel, tm=tm),
        grid=(n_tok // tm,),
        in_specs=[
            pl.BlockSpec((tm, D_MODEL), tok),
            pl.BlockSpec((tm, W_A), tok),
            pl.BlockSpec((tm, W_B), tok),
            pl.BlockSpec((None, N_MOD, D_MODEL), mod_idx),
            pl.BlockSpec((1, D_MODEL), const2),
            pl.BlockSpec((1, D_MODEL), const2),
            resident(w_out), resident(w_gate), resident(w_up), resident(w_down),
        ],
        out_specs=pl.BlockSpec((tm, D_MODEL), tok),
        out_shape=jax.ShapeDtypeStruct((n_tok, D_MODEL), F32),
        scratch_shapes=[pltpu.VMEM((tm, D_FF), BF16)],
        compiler_params=pltpu.CompilerParams(vmem_limit_bytes=VMEM_LIMIT),
        name="outffn",
    )(x2d, ya, yb, mod, n2, fg, w_out, w_gate, w_up, w_down)


def _trunk_layer(x, mod, row0, row_stride, s_f0, s_b0, p, final_g):
    bsz, t, _ = x.shape
    x2d = x.reshape(bsz * t, D_MODEL)
    qkv, z, ba, yb = _inproj(x2d, mod, p["g1"], p["w_in"], p["a_row"], p["dt_row"], p["ln_g"], p["ln_b"], p["w_s"],
                             p["b_s_t"], layer=p["layer"], seq=t, row0=row0, row_stride=row_stride, tm=IN_TM)
    ya, s_f, s_b = _deltanet(qkv.reshape(bsz, t, 3 * W_A), z.reshape(bsz, t, W_A), ba.reshape(bsz, t, GATE_W),
                             p["conv_w"], p["dn_g"], s_f0, s_b0, hps=H_A if t <= DN_SHORT else 2)
    out = _outffn(x2d, ya.reshape(bsz * t, W_A), yb, mod, p["g2"], final_g, p["w_out"], p["w_gate"], p["w_up"],
                  p["w_down"], seq=t, row0=row0, row_stride=row_stride,
                  tm=min(FFN_TM, bsz * t // MIN_STEPS))
    return out.reshape(bsz, t, D_MODEL), s_f, s_b


def kernel(x_prompt, x_sample, state_fwd, state_bwd, c, c_ctx, w_ada, b_ada, norm1_g, norm2_g, w_in, conv_w,
           a_log, dt_bias, dn_norm_g, sgu_ln_g, sgu_ln_b, sgu_w, sgu_b, w_out, w_gate, w_up, w_down, final_g):
    depth = w_ada.shape[0]
    assert depth == 1, "the final RMSNorm is fused into the last layer; only depth 1 is supported"
    n_lat = c.shape[0]
    assert 1 + n_lat <= MOD_ROWS
    cc = jnp.zeros((MOD_ROWS, D_MODEL), F32).at[0].set(c_ctx).at[1:1 + n_lat].set(c)
    s_zero = jnp.zeros((x_prompt.shape[0], H_A, DK, DV), F32)
    xp, xs = x_prompt, x_sample
    new_f, new_b = [], []
    gate_pad = jnp.zeros((2 * H_A,), F32)
    for l in range(depth):
        p = dict(
            g1=norm1_g[l].reshape(1, D_MODEL), g2=norm2_g[l].reshape(1, D_MODEL),
            w_in=w_in, layer=l,
            ln_g=sgu_ln_g[l].reshape(1, W_B), ln_b=sgu_ln_b[l].reshape(1, W_B),
            w_s=sgu_w[l].astype(BF16), b_s_t=jnp.transpose(sgu_b[l]),
            conv_w=conv_w[l],
            a_row=jnp.pad(jnp.concatenate([gate_pad, a_log[l].reshape(-1)]), (0, GATE_W - 4 * H_A)).reshape(1, GATE_W),
            dt_row=jnp.pad(jnp.concatenate([gate_pad, dt_bias[l].reshape(-1)]), (0, GATE_W - 4 * H_A)).reshape(1, GATE_W),
            dn_g=dn_norm_g[l].reshape(1, DV),
            w_out=w_out[l].astype(BF16), w_gate=w_gate[l].astype(BF16), w_up=w_up[l].astype(BF16),
            w_down=w_down[l].astype(BF16),
        )
        mod = _ada(cc, w_ada, b_ada[l], l).reshape(MOD_ROWS, N_MOD, D_MODEL)
        fg = final_g.reshape(1, D_MODEL)
        xp, sf, sb = _trunk_layer(xp, mod, 0, 0, s_zero, s_zero, p, fg)
        new_f.append(sf)
        new_b.append(sb)
        xs, _, _ = _trunk_layer(xs, mod, 1, 1, state_fwd[:, l], state_bwd[:, l], p, fg)
    return (xp, xs, jnp.stack(new_f, axis=1), jnp.stack(new_b, axis=1))
```

```python
import functools

import jax
import jax.numpy as jnp
from jax import lax
from jax.experimental import pallas as pl
from jax.experimental.pallas import tpu as pltpu

F32 = jnp.float32
BF16 = jnp.bfloat16

D_MODEL = 1024
W_A = D_MODEL // 2
DK = 128
DV = 128
H_A = W_A // DV
W_B = D_MODEL - W_A
WG = 128
G_B = W_B // WG
SGU_CHUNK = 128
CONV_K = 5
DN_CHUNK = 128
D_FF = 2816
N_MOD = 6
EPS = 1e-6

MOD_ROWS = 16
GATE_W = 128
LANES = 128
INV_BASE = 16
FF_CHUNK = 256
IN_SUB = 256
IN_TM = 1024
FFN_SUB = 256
FFN_TM = 1024
MIN_STEPS = 8
DN_SHORT = 512
P1_BATCH = 8
VMEM_LIMIT = 56 * 1024 * 1024
VMEM_LIMIT_DN = 58 * 1024 * 1024


def _sigmoid(x):
    return 0.5 + 0.5 * jnp.tanh(0.5 * x)


def _silu(x):
    hx = 0.5 * x
    return hx + hx * jnp.tanh(hx)


def _gelu_tanh(x):
    c = 0.7978845608028654
    return 0.5 * x * (1.0 + jnp.tanh(c * (x + 0.044715 * (x * x * x))))


def _softplus(x):
    return jnp.maximum(x, 0.0) + jnp.log(1.0 + jnp.exp(-jnp.abs(x)))


def _dot(a, b):
    return jnp.dot(a, b, preferred_element_type=F32)


def _dot_nt(a, b):
    return lax.dot_general(a, b, (((1,), (1,)), ((), ())), preferred_element_type=F32)


def _dot_tn(a, b):
    return lax.dot_general(a, b, (((0,), (0,)), ((), ())), preferred_element_type=F32)


def _split3(x):
    hi = x.astype(BF16)
    r1 = x - hi.astype(F32)
    mid = r1.astype(BF16)
    lo = (r1 - mid.astype(F32)).astype(BF16)
    return hi, mid, lo


def _dot_exact_lhs(a_bf16, x):
    hi, mid, lo = _split3(x)
    return _dot(a_bf16, hi) + _dot(a_bf16, mid) + _dot(a_bf16, lo)


def _ada_kernel(c_ref, w_ref, b_ref, o_ref):
    s = _silu(c_ref[...])
    o_ref[...] = _dot(s.astype(BF16), w_ref[...].astype(BF16)) + b_ref[...]


def _ada(cc, w_ada, b_ada, layer):
    n_out = w_ada.shape[2]
    bn = D_MODEL
    return pl.pallas_call(
        _ada_kernel,
        grid=(n_out // bn,),
        in_specs=[
            pl.BlockSpec((MOD_ROWS, D_MODEL), lambda j: (0, 0)),
            pl.BlockSpec((None, D_MODEL, bn), lambda j: (layer, 0, j)),
            pl.BlockSpec((1, bn), lambda j: (0, j)),
        ],
        out_specs=pl.BlockSpec((MOD_ROWS, bn), lambda j: (0, j)),
        out_shape=jax.ShapeDtypeStruct((MOD_ROWS, n_out), F32),
        compiler_params=pltpu.CompilerParams(vmem_limit_bytes=VMEM_LIMIT),
        name="ada",
    )(cc, w_ada, b_ada.reshape(1, n_out))


def _inproj_kernel(x_ref, mod_ref, g1_ref, w_ref, arow_ref, dtrow_ref, lng_ref, lnb_ref, ws_ref, bs_ref,
                   qkv_ref, z_ref, ba_ref, yb_ref, wm_ref, wba_ref, *, tm):
    n_main = 4 * W_A + 2 * W_B

    @pl.when(pl.program_id(0) == 0)
    def _():
        for c in range(0, n_main, W_A):
            wm_ref[:, c:c + W_A] = w_ref[:, c:c + W_A].astype(BF16)
        wba_ref[...] = jnp.zeros(wba_ref.shape, BF16)
        wba_ref[:, 0:4 * H_A] = w_ref[:, n_main:n_main + 4 * H_A].astype(BF16)

    subs = [slice(r, r + IN_SUB) for r in range(0, tm, IN_SUB)]
    ri = lax.broadcasted_iota(jnp.int32, (IN_SUB, IN_SUB), 0)
    ci = lax.broadcasted_iota(jnp.int32, (IN_SUB, IN_SUB), 1)
    same_chunk = (ri // DN_CHUNK) == (ci // DN_CHUNK)
    prefix_op = (same_chunk & (ri >= ci)).astype(BF16)

    def norm(r):
        x = x_ref[r, :]
        ms = jnp.mean(x * x, axis=-1, keepdims=True)
        xn = x * lax.rsqrt(ms + EPS) * g1_ref[...]
        return (xn * (1.0 + mod_ref[1:2, :]) + mod_ref[0:1, :]).astype(BF16)

    def matmul_tasks(r, hb):
        def qkv_piece(j):
            def run():
                l = slice(j * W_A, (j + 1) * W_A)
                qkv_ref[r, l] = _dot(hb, wm_ref[:, l]).astype(BF16)
            return run

        def z_piece():
            z_ref[r, :] = _dot(hb, wm_ref[:, 3 * W_A:4 * W_A]).astype(BF16)

        def gate_piece():
            logit = _dot(hb, wba_ref[...])
            lane = lax.broadcasted_iota(jnp.int32, logit.shape, 1)
            g_all = -jnp.exp(arow_ref[...]) * _softplus(logit + dtrow_ref[...])
            cum_f = _dot_exact_lhs(prefix_op, g_all)
            tot = jnp.concatenate(
                [jnp.broadcast_to(cum_f[c + DN_CHUNK - 1:c + DN_CHUNK, :], (DN_CHUNK, GATE_W))
                 for c in range(0, IN_SUB, DN_CHUNK)], axis=0)
            cum_b = tot - cum_f + g_all
            ba_ref[r, :] = jnp.where(lane < 2 * H_A, _sigmoid(logit), jnp.where(lane < 3 * H_A, cum_f, cum_b))

        return [qkv_piece(0), qkv_piece(1), qkv_piece(2), z_piece, gate_piece]

    def sgu_tasks(r, st):
        def gelu_u():
            st["u"] = _gelu_tanh(st["u"])

        def gelu_v():
            st["v"] = _gelu_tanh(st["v"])

        def layernorm_v():
            vb = st["v"]
            mu = jnp.mean(vb, axis=-1, keepdims=True)
            vc = vb - mu
            var = jnp.mean(vc * vc, axis=-1, keepdims=True)
            st["v"] = (vc * lax.rsqrt(var + EPS) * lng_ref[...] + lnb_ref[...]).astype(BF16)

        def mix(c):
            def run():
                rr = slice(c * SGU_CHUNK, (c + 1) * SGU_CHUNK)
                ro = slice(r.start + c * SGU_CHUNK, r.start + (c + 1) * SGU_CHUNK)
                for g in range(G_B):
                    l = slice(g * WG, (g + 1) * WG)
                    mixed = _dot(ws_ref[g], st["v"][rr, l]) + bs_ref[:, g:g + 1]
                    yb_ref[ro, l] = (st["u"][rr, l] * mixed).astype(BF16)
            return run

        return [gelu_u, gelu_v, layernorm_v] + [mix(c) for c in range(IN_SUB // SGU_CHUNK)]

    hb = norm(subs[0])
    for i, r in enumerate(subs):
        st = {"u": _dot(hb, wm_ref[:, 4 * W_A:4 * W_A + W_B]),
              "v": _dot(hb, wm_ref[:, 4 * W_A + W_B:4 * W_A + 2 * W_B])}
        mm = matmul_tasks(r, hb)
        ew = sgu_tasks(r, st)
        nxt = {}
        if i + 1 < len(subs):
            ew.append(lambda i=i: nxt.update(hb=norm(subs[i + 1])))
        for k in range(max(len(mm), len(ew))):
            if k < len(mm):
                mm[k]()
            if k < len(ew):
                ew[k]()
        hb = nxt.get("hb")


def _inproj(x2d, mod, g1, w_in, a_row, dt_row, ln_g, ln_b, w_s, b_s_t, *, layer, seq, row0, row_stride, tm):
    n_tok = x2d.shape[0]
    assert row_stride == 0 or seq % tm == 0

    def mod_idx(i):
        return (row0 + ((i * tm) // seq) * row_stride, 0, 0)

    const2 = lambda i: (0, 0)
    tok = lambda i: (i, 0)
    return pl.pallas_call(
        functools.partial(_inproj_kernel, tm=tm),
        grid=(n_tok // tm,),
        in_specs=[
            pl.BlockSpec((tm, D_MODEL), tok),
            pl.BlockSpec((None, N_MOD, D_MODEL), mod_idx),
            pl.BlockSpec((1, D_MODEL), const2),
            pl.BlockSpec((None,) + w_in.shape[1:], lambda i: (layer, 0, 0), pipeline_mode=pl.Buffered(1)),
            pl.BlockSpec((1, GATE_W), const2),
            pl.BlockSpec((1, GATE_W), const2),
            pl.BlockSpec((1, W_B), const2),
            pl.BlockSpec((1, W_B), const2),
            pl.BlockSpec(w_s.shape, lambda i: (0, 0, 0)),
            pl.BlockSpec(b_s_t.shape, const2),
        ],
        out_specs=[
            pl.BlockSpec((tm, 3 * W_A), tok),
            pl.BlockSpec((tm, W_A), tok),
            pl.BlockSpec((tm, GATE_W), tok),
            pl.BlockSpec((tm, W_B), tok),
        ],
        out_shape=[
            jax.ShapeDtypeStruct((n_tok, 3 * W_A), BF16),
            jax.ShapeDtypeStruct((n_tok, W_A), BF16),
            jax.ShapeDtypeStruct((n_tok, GATE_W), F32),
            jax.ShapeDtypeStruct((n_tok, W_B), BF16),
        ],
        scratch_shapes=[pltpu.VMEM((D_MODEL, 4 * W_A + 2 * W_B), BF16),
                        pltpu.VMEM((D_MODEL, GATE_W), BF16)],
        compiler_params=pltpu.CompilerParams(vmem_limit_bytes=VMEM_LIMIT),
        name="inproj",
    )(x2d, mod, g1, w_in, a_row, dt_row, ln_g, ln_b, w_s, b_s_t)


def _compress_diag_blocks(a):
    n = a.shape[0]
    lane = lax.broadcasted_iota(jnp.int32, (INV_BASE, n), 1)
    out = jnp.where((lane // INV_BASE) == 0, a[0:INV_BASE, :], 0.0)
    for rb in range(1, n // INV_BASE):
        out = out + jnp.where((lane // INV_BASE) == rb, a[rb * INV_BASE:(rb + 1) * INV_BASE, :], 0.0)
    return out


def _tri_inv_stages(get_diag, get_mats, out):
    n = DN_CHUNK
    nblk = n // INV_BASE
    per = LANES // n
    assert per >= 1

    def eliminate():
        diag_blocks = get_diag()
        assert len(diag_blocks) % per == 0
        lane1 = lax.broadcasted_iota(jnp.int32, (INV_BASE, n), 1)
        lane2 = lax.broadcasted_iota(jnp.int32, (INV_BASE, per * n), 1)
        sub2 = lax.broadcasted_iota(jnp.int32, (INV_BASE, per * n), 0)
        packed = [diag_blocks[p] if per == 1 else
                  jnp.concatenate([diag_blocks[per * p + q] for q in range(per)], axis=1)
                  for p in range(len(diag_blocks) // per)]
        eye = ((lane2 % INV_BASE) == sub2).astype(F32)
        xs = [eye for _ in packed]
        grp = (lane2 // INV_BASE) * INV_BASE
        for j in range(INV_BASE - 1):
            cols = [jnp.take_along_axis(nc, grp + j, axis=1) for nc in packed]
            xs = [x - c * x[j:j + 1, :] for x, c in zip(xs, cols)]

        def expand(xc):
            xc = xc.astype(BF16)
            zero_rows = jnp.zeros_like(xc)
            return jnp.concatenate([jnp.where((lane1 // INV_BASE) == rb, xc, zero_rows) for rb in range(nblk)],
                                   axis=0)

        out["inv"] = [expand(x[:, q * n:(q + 1) * n]) for x in xs for q in range(per)]

    def merge(bs):
        def run():
            row = lax.broadcasted_iota(jnp.int32, (n, n), 0)
            col = lax.broadcasted_iota(jnp.int32, (n, n), 1)
            zero = jnp.zeros((n, n), BF16)
            cmask = (((row // (2 * bs)) == (col // (2 * bs))) & (((row // bs) % 2) == 1)
                     & (((col // bs) % 2) == 0))
            ds = out["inv"]
            cs = [jnp.where(cmask, a, zero) for a in get_mats()]
            t1 = [_dot(d, c) for d, c in zip(ds, cs)]
            t2 = [_dot(t.astype(BF16), d) for t, d in zip(t1, ds)]
            out["inv"] = [jnp.where(cmask, (-t).astype(BF16), d) for d, t in zip(ds, t2)]
        return run

    levels = []
    bs = INV_BASE
    while bs < n:
        levels.append(merge(bs))
        bs *= 2
    return [eliminate] + levels


def _dn_kernel(q_ref, k_ref, v_ref, z_ref, ba_ref, cwq_ref, cwk_ref, cwv_ref,
               ng_ref, sf0_ref, sb0_ref,
               ya_ref, sf_ref, sb_ref,
               padq_ref, padk_ref, padv_ref, qs_ref, ks_ref, vs_ref,
               r_ref, mq_ref, et_ref, o_ref, *prep_refs, seq, hps):
    t = seq
    c_len = DN_CHUNK
    n = t // c_len
    step = pl.program_id(0)
    head0 = (jnp.maximum(step - 1, 0) % (H_A // hps)) * hps
    slot_new = step % 2
    slot_cur = 1 - slot_new
    rb = min(t, 256)
    heads = [slice(g * DK, (g + 1) * DK) for g in range(hps)]

    zeros8 = jnp.zeros((8, DK), F32)
    for src_ref, pad_ref in ((q_ref, padq_ref), (k_ref, padk_ref), (v_ref, padv_ref)):
        for g, l in enumerate(heads):
            pad_ref[g, 0:8, :] = zeros8
            pad_ref[g, 8 + t:16 + t, :] = zeros8
            for r in range(0, t, rb):
                pad_ref[g, 8 + r:8 + r + rb, :] = src_ref[r:r + rb, l].astype(F32)

    def conv_block(i):
        r0 = pl.multiple_of(i * c_len, c_len)
        for pad_ref, cw_ref, dst_ref, l2, scale in ((padq_ref, cwq_ref, qs_ref, True, DK ** -0.5),
                                                    (padk_ref, cwk_ref, ks_ref, True, 1.0),
                                                    (padv_ref, cwv_ref, vs_ref, False, 1.0)):
            for g, l in enumerate(heads):
                acc = cw_ref[0:1, l] * pad_ref[g, pl.ds(r0 + 6, c_len), :]
                for j in range(1, CONV_K):
                    acc = acc + cw_ref[j:j + 1, l] * pad_ref[g, pl.ds(r0 + 6 + j, c_len), :]
                y = _silu(acc)
                if l2:
                    y = y * (lax.rsqrt(jnp.sum(y * y, axis=-1, keepdims=True) + EPS) * scale)
                dst_ref[slot_new, g, pl.ds(r0, c_len), :] = y

    @pl.when(step == 0)
    def _():
        def body(i, carry):
            conv_block(i)
            return carry
        lax.fori_loop(0, n, body, 0)

    @pl.when(step > 0)
    def _():
        _dn_group(z_ref, ba_ref, ng_ref, sf0_ref, sb0_ref, ya_ref, sf_ref, sb_ref,
                  qs_ref.at[slot_cur], ks_ref.at[slot_cur], vs_ref.at[slot_cur],
                  r_ref, mq_ref, et_ref, o_ref, prep_refs, conv_block, head0, t, hps)


def _dn_group(z_ref, ba_ref, ng_ref, sf0_ref, sb0_ref, ya_ref, sf_ref, sb_ref, qs_ref, ks_ref, vs_ref,
              r_ref, mq_ref, et_ref, o_ref, prep_refs, conv_block, head0, t, hps):
    c_len = DN_CHUNK
    n = t // c_len
    rb = min(t, 256)

    row = lax.broadcasted_iota(jnp.int32, (c_len, c_len), 0)
    col = lax.broadcasted_iota(jnp.int32, (c_len, c_len), 1)
    lower_incl = row >= col
    upper_incl = row <= col
    lower_strict = row > col
    gate_lane = [jnp.full((c_len, GATE_W), head0 + g, jnp.int32) for g in range(hps)]
    nb = max(1, min(P1_BATCH // hps, n))
    assert n % nb == 0
    n_batch = n // nb
    chunks = [(g, i) for g in range(hps) for i in range(nb)]
    items = [(g, i, d) for g, i in chunks for d in range(2)]
    pa16_ref, pdiag_ref, prhs_ref, pqkd_ref, pkdt_ref, pqe_ref, pb_ref = prep_refs

    def rows(it, stride):
        return [pl.multiple_of((it * nb + i) * stride, stride) for i in range(nb)]

    def prep_tasks(it, slot):
        r64, r8 = rows(it, c_len), rows(it, 8)
        sh = {}

        def head():
            sh["q"] = {gi: qs_ref[gi[0], pl.ds(r64[gi[1]], c_len), :] for gi in chunks}
            sh["k"] = {gi: ks_ref[gi[0], pl.ds(r64[gi[1]], c_len), :] for gi in chunks}
            kb = {gi: sh["k"][gi].astype(BF16) for gi in chunks}
            sh["kk"] = {gi: _dot_nt(kb[gi], kb[gi]) for gi in chunks}
            sh["qk"] = {gi: _dot_nt(sh["q"][gi].astype(BF16), kb[gi]) for gi in chunks}

        def item(idx, g, i, d):
            def run():
                q, k = sh["q"][g, i], sh["k"][g, i]
                gates = ba_ref[pl.ds(r64[i], c_len), :]
                b = jnp.take_along_axis(gates, gate_lane[g] + d * H_A, axis=1)
                cum = jnp.take_along_axis(gates, gate_lane[g] + (2 + d) * H_A, axis=1)
                dm = cum[:, 0:c_len] - cum[:, 0:c_len].T
                dec_sys = jnp.where(lower_incl, jnp.exp(dm if d == 0 else -dm), 0.0)
                dec_qk = dec_sys if d == 0 else jnp.where(upper_incl, jnp.exp(dm), 0.0)
                a = jnp.where(lower_strict, b[:, 0:c_len] * sh["kk"][g, i] * dec_sys, 0.0)
                e_col = jnp.exp(cum)
                v = vs_ref[g, pl.ds(r64[i], c_len), :]
                if d == 0:
                    rhs = jnp.concatenate([v * b, k * (b * e_col)], axis=1)
                else:
                    rhs = jnp.concatenate([v, k * e_col], axis=1)
                    pb_ref[slot, idx // 2] = b
                tot = cum[c_len - 1:c_len, :] if d == 0 else cum[0:1, :]
                et_ref[g, d, pl.ds(r8[i], 8), :] = jnp.exp(jnp.broadcast_to(tot, (8, DK)))
                pa16_ref[slot, idx] = a.astype(BF16)
                pdiag_ref[slot, idx] = _compress_diag_blocks(a)
                prhs_ref[slot, idx] = rhs.astype(BF16)
                pqkd_ref[slot, idx] = (sh["qk"][g, i] * dec_qk).astype(BF16)
                pkdt_ref[slot, idx] = (k * jnp.exp(tot - cum)).T.astype(BF16)
                pqe_ref[slot, idx] = q * e_col
            return run

        return [head] + [item(idx, g, i, d) for idx, (g, i, d) in enumerate(items)]

    def solve_tasks(it, slot):
        r64, r128, r192 = rows(it, c_len), rows(it, DK), rows(it, DK + c_len)
        st = {}
        inverse = _tri_inv_stages(lambda: [pdiag_ref[slot, x] for x in range(len(items))],
                                  lambda: [pa16_ref[slot, x] for x in range(len(items))], st)

        def uw_stage():
            uw = []
            for idx, (g, i, d) in enumerate(items):
                if d == 0:
                    uw.append(_dot(st["inv"][idx], prhs_ref[slot, idx]))
                else:
                    b = pb_ref[slot, idx // 2]
                    uw.append(jnp.concatenate([b, b], axis=1) * _dot_tn(st["inv"][idx], prhs_ref[slot, idx]))
            st["uwb"] = [x.astype(BF16) for x in uw]

        def out_stage():
            kuw = [_dot(pkdt_ref[slot, x], u) for x, u in enumerate(st["uwb"])]
            quw = [_dot(pqkd_ref[slot, x], u) for x, u in enumerate(st["uwb"])]
            for idx, (g, i, d) in enumerate(items):
                r_ref[g, d, pl.ds(r128[i], DK), :] = kuw[idx][:, 0:DV]
                mq_ref[g, d, pl.ds(r192[i], DK), :] = (-kuw[idx][:, DV:DV + DK]).astype(BF16)
                mq_ref[g, d, pl.ds(r192[i] + DK, c_len), :] = (pqe_ref[slot, idx]
                                                               - quw[idx][:, DV:DV + DK]).astype(BF16)
                o_ref[g, d, pl.ds(r64[i], c_len), :] = quw[idx][:, 0:DV]

        return inverse + [uw_stage, out_stage]

    def emit(solve, prep):
        per = -(-len(prep) // max(len(solve), 1)) if solve else len(prep)
        p = 0
        for s in solve:
            s()
            for task in prep[p:p + per]:
                task()
            p += per
        for task in prep[p:]:
            task()

    emit([], prep_tasks(0, 0))

    def phase1(it, carry):
        slot = it % 2
        emit(solve_tasks(it, slot), prep_tasks(it + 1, 1 - slot))
        return carry

    lax.fori_loop(0, n_batch - 1, phase1, 0)
    emit(solve_tasks(n_batch - 1, (n_batch - 1) % 2), [])


    def phase2(i, carry):
        new = []
        for g in range(hps):
            for d in range(2):
                s = carry[2 * g + d]
                c = i if d == 0 else n - 1 - i
                r64 = pl.multiple_of(c * c_len, c_len)
                r128 = pl.multiple_of(c * DK, DK)
                r192 = pl.multiple_of(c * (DK + c_len), DK + c_len)
                ms = _dot(mq_ref[g, d, pl.ds(r192, DK + c_len), :], s.astype(BF16))
                et = et_ref[g, d, pl.ds(pl.multiple_of(c * 8, 8), 8), :]
                new.append(s * et[0:1, :] + r_ref[g, d, pl.ds(r128, DK), :] + ms[0:DK])
                o_ref[g, d, pl.ds(r64, c_len), :] = o_ref[g, d, pl.ds(r64, c_len), :] + ms[DK:DK + c_len]
        conv_block(i)
        return tuple(new)

    init = []
    for g in range(hps):
        init += [sf0_ref[g], sb0_ref[g]]
    fin = lax.fori_loop(0, n, phase2, tuple(init))
    for g in range(hps):
        sf_ref[g] = fin[2 * g]
        sb_ref[g] = fin[2 * g + 1]

    for g in range(hps):
        l = slice(g * DV, (g + 1) * DV)
        for r in range(0, t, rb):
            o = o_ref[g, 0, r:r + rb, :] + o_ref[g, 1, r:r + rb, :]
            ms = jnp.mean(o * o, axis=-1, keepdims=True)
            y = o * lax.rsqrt(ms + EPS) * ng_ref[...]
            ya_ref[r:r + rb, l] = (y * _silu(z_ref[r:r + rb, l].astype(F32))).astype(BF16)


def _deltanet(qkv, z, ba, conv_w, norm_g, s_f0, s_b0, *, hps):
    bsz, t, _ = qkv.shape
    n = t // DN_CHUNK
    hb = H_A // hps
    n_grp = bsz * hb
    prep = lambda s: jnp.minimum(s, n_grp - 1)
    run = lambda s: jnp.maximum(s - 1, 0)
    prep_col = lambda off: (lambda s: (prep(s) // hb, 0, off + prep(s) % hb))
    prep_cw = lambda off: (lambda s: (0, off + prep(s) % hb))
    run_col = lambda s: (run(s) // hb, 0, run(s) % hb)
    run_state = lambda s: (run(s) // hb, run(s) % hb, 0, 0)
    w = hps * DK
    n_items = 2 * hps * max(1, min(P1_BATCH // hps, n))
    return pl.pallas_call(
        functools.partial(_dn_kernel, seq=t, hps=hps),
        grid=(n_grp + 1,),
        in_specs=[
            pl.BlockSpec((None, t, w), prep_col(0)),
            pl.BlockSpec((None, t, w), prep_col(hb)),
            pl.BlockSpec((None, t, w), prep_col(2 * hb)),
            pl.BlockSpec((None, t, w), run_col),
            pl.BlockSpec((None, t, GATE_W), lambda s: (run(s) // hb, 0, 0)),
            pl.BlockSpec((CONV_K, w), prep_cw(0)),
            pl.BlockSpec((CONV_K, w), prep_cw(hb)),
            pl.BlockSpec((CONV_K, w), prep_cw(2 * hb)),
            pl.BlockSpec((1, DV), lambda s: (0, 0)),
            pl.BlockSpec((None, hps, DK, DV), run_state),
            pl.BlockSpec((None, hps, DK, DV), run_state),
        ],
        out_specs=[
            pl.BlockSpec((None, t, w), run_col),
            pl.BlockSpec((None, hps, DK, DV), run_state),
            pl.BlockSpec((None, hps, DK, DV), run_state),
        ],
        out_shape=[
            jax.ShapeDtypeStruct((bsz, t, W_A), BF16),
            jax.ShapeDtypeStruct((bsz, H_A, DK, DV), F32),
            jax.ShapeDtypeStruct((bsz, H_A, DK, DV), F32),
        ],
        scratch_shapes=[
            pltpu.VMEM((hps, t + 16, DK), F32),
            pltpu.VMEM((hps, t + 16, DK), F32),
            pltpu.VMEM((hps, t + 16, DV), F32),
            pltpu.VMEM((2, hps, t, DK), F32),
            pltpu.VMEM((2, hps, t, DK), F32),
            pltpu.VMEM((2, hps, t, DV), F32),
            pltpu.VMEM((hps, 2, n * DK, DV), F32),
            pltpu.VMEM((hps, 2, n * (DK + DN_CHUNK), DK), BF16),
            pltpu.VMEM((hps, 2, n * 8, DV), F32),
            pltpu.VMEM((hps, 2, t, DV), F32),
            pltpu.VMEM((2, n_items, DN_CHUNK, DN_CHUNK), BF16),
            pltpu.VMEM((2, n_items, INV_BASE, DN_CHUNK), F32),
            pltpu.VMEM((2, n_items, DN_CHUNK, DV + DK), BF16),
            pltpu.VMEM((2, n_items, DN_CHUNK, DN_CHUNK), BF16),
            pltpu.VMEM((2, n_items, DK, DN_CHUNK), BF16),
            pltpu.VMEM((2, n_items, DN_CHUNK, DK), F32),
            pltpu.VMEM((2, n_items // 2, DN_CHUNK, DK), F32),
        ],
        compiler_params=pltpu.CompilerParams(vmem_limit_bytes=VMEM_LIMIT_DN),
        name="deltanet",
    )(qkv, qkv, qkv, z, ba, conv_w, conv_w, conv_w, norm_g, s_f0, s_b0)


def _outffn_kernel(x_ref, ya_ref, yb_ref, mod_ref, n2_ref, fg_ref, wo_ref, wg_ref, wu_ref, wd_ref, o_ref,
                   act_ref, *, tm):
    subs = [slice(r, r + FFN_SUB) for r in range(0, tm, FFN_SUB)]
    hb = []
    for r in subs:
        y = _dot(ya_ref[r, :], wo_ref[0:W_A, :]) + _dot(yb_ref[r, :], wo_ref[W_A:D_MODEL, :])
        x1 = x_ref[r, :] + mod_ref[2:3, :] * y
        o_ref[r, :] = x1
        ms = jnp.mean(x1 * x1, axis=-1, keepdims=True)
        h = (x1 * lax.rsqrt(ms + EPS) * n2_ref[...]) * (1.0 + mod_ref[4:5, :]) + mod_ref[3:4, :]
        hb.append(h.astype(BF16))
    for c in range(D_FF // FF_CHUNK):
        l = slice(c * FF_CHUNK, (c + 1) * FF_CHUNK)
        for r, h in zip(subs, hb):
            act_ref[r, l] = (_silu(_dot(h, wg_ref[:, l])) * _dot(h, wu_ref[:, l])).astype(BF16)
    for r in subs:
        x2 = o_ref[r, :] + mod_ref[5:6, :] * _dot(act_ref[r, :], wd_ref[...])
        ms2 = jnp.mean(x2 * x2, axis=-1, keepdims=True)
        o_ref[r, :] = x2 * lax.rsqrt(ms2 + EPS) * fg_ref[...]


def _outffn(x2d, ya, yb, mod, n2, fg, w_out, w_gate, w_up, w_down, *, seq, row0, row_stride, tm):
    n_tok = x2d.shape[0]
    assert row_stride == 0 or seq % tm == 0

    def mod_idx(i):
        return (row0 + ((i * tm) // seq) * row_stride, 0, 0)

    const2 = lambda i: (0, 0)
    tok = lambda i: (i, 0)
    resident = lambda w: pl.BlockSpec(w.shape, const2, pipeline_mode=pl.Buffered(1))
    return pl.pallas_call(
        functools.partial(_outffn_kernel, tm=tm),
        grid=(n_tok // tm,),
        in_specs=[
            pl.BlockSpec((tm, D_MODEL), tok),
            pl.BlockSpec((tm, W_A), tok),
            pl.BlockSpec((tm, W_B), tok),
            pl.BlockSpec((None, N_MOD, D_MODEL), mod_idx),
            pl.BlockSpec((1, D_MODEL), const2),
            pl.BlockSpec((1, D_MODEL), const2),
            resident(w_out), resident(w_gate), resident(w_up), resident(w_down),
        ],
        out_specs=pl.BlockSpec((tm, D_MODEL), tok),
        out_shape=jax.ShapeDtypeStruct((n_tok, D_MODEL), F32),
        scratch_shapes=[pltpu.VMEM((tm, D_FF), BF16)],
        compiler_params=pltpu.CompilerParams(vmem_limit_bytes=VMEM_LIMIT),
        name="outffn",
    )(x2d, ya, yb, mod, n2, fg, w_out, w_gate, w_up, w_down)


def _trunk_layer(x, mod, row0, row_stride, s_f0, s_b0, p, final_g):
    bsz, t, _ = x.shape
    x2d = x.reshape(bsz * t, D_MODEL)
    qkv, z, ba, yb = _inproj(x2d, mod, p["g1"], p["w_in"], p["a_row"], p["dt_row"], p["ln_g"], p["ln_b"], p["w_s"],
                             p["b_s_t"], layer=p["layer"], seq=t, row0=row0, row_stride=row_stride, tm=IN_TM)
    ya, s_f, s_b = _deltanet(qkv.reshape(bsz, t, 3 * W_A), z.reshape(bsz, t, W_A), ba.reshape(bsz, t, GATE_W),
                             p["conv_w"], p["dn_g"], s_f0, s_b0, hps=H_A if t <= DN_SHORT else 2)
    out = _outffn(x2d, ya.reshape(bsz * t, W_A), yb, mod, p["g2"], final_g, p["w_out"], p["w_gate"], p["w_up"],
                  p["w_down"], seq=t, row0=row0, row_stride=row_stride,
                  tm=min(FFN_TM, bsz * t // MIN_STEPS))
    return out.reshape(bsz, t, D_MODEL), s_f, s_b


def kernel(x_prompt, x_sample, state_fwd, state_bwd, c, c_ctx, w_ada, b_ada, norm1_g, norm2_g, w_in, conv_w,
           a_log, dt_bias, dn_norm_g, sgu_ln_g, sgu_ln_b, sgu_w, sgu_b, w_out, w_gate, w_up, w_down, final_g):
    depth = w_ada.shape[0]
    assert depth == 1, "the final RMSNorm is fused into the last layer; only depth 1 is supported"
    n_lat = c.shape[0]
    assert 1 + n_lat <= MOD_ROWS
    cc = jnp.zeros((MOD_ROWS, D_MODEL), F32).at[0].set(c_ctx).at[1:1 + n_lat].set(c)
    s_zero = jnp.zeros((x_prompt.shape[0], H_A, DK, DV), F32)
    xp, xs = x_prompt, x_sample
    new_f, new_b = [], []
    gate_pad = jnp.zeros((2 * H_A,), F32)
    for l in range(depth):
        p = dict(
            g1=norm1_g[l].reshape(1, D_MODEL), g2=norm2_g[l].reshape(1, D_MODEL),
            w_in=w_in, layer=l,
            ln_g=sgu_ln_g[l].reshape(1, W_B), ln_b=sgu_ln_b[l].reshape(1, W_B),
            w_s=sgu_w[l].astype(BF16), b_s_t=jnp.transpose(sgu_b[l]),
            conv_w=conv_w[l],
            a_row=jnp.pad(jnp.concatenate([gate_pad, a_log[l].reshape(-1)]), (0, GATE_W - 4 * H_A)).reshape(1, GATE_W),
            dt_row=jnp.pad(jnp.concatenate([gate_pad, dt_bias[l].reshape(-1)]), (0, GATE_W - 4 * H_A)).reshape(1, GATE_W),
            dn_g=dn_norm_g[l].reshape(1, DV),
            w_out=w_out[l].astype(BF16), w_gate=w_gate[l].astype(BF16), w_up=w_up[l].astype(BF16),
            w_down=w_down[l].astype(BF16),
        )
        mod = _ada(cc, w_ada, b_ada[l], l).reshape(MOD_ROWS, N_MOD, D_MODEL)
        fg = final_g.reshape(1, D_MODEL)
        xp, sf, sb = _trunk_layer(xp, mod, 0, 0, s_zero, s_zero, p, fg)
        new_f.append(sf)
        new_b.append(sb)
        xs, _, _ = _trunk_layer(xs, mod, 1, 1, state_fwd[:, l], state_bwd[:, l], p, fg)
    return (xp, xs, jnp.stack(new_f, axis=1), jnp.stack(new_b, axis=1))
```
